```python
import jax, jax.numpy as jnp
from jax import lax
import numpy as np

D_MODEL = 1024
BATCH = 16
SEQ = 2048
DEPTH = 2

HEAD_DIM = 64
MOBA_HEADS = 8
MOBA_BLOCK = 256
MOBA_TOPK = 3
MOBA_Q_CHUNK = 16
DIL_PATTERNS = ((128, 1), (512, 4), (2048, 16))
DIL_HEADS_PER_GROUP = 4
N_DIL_GROUPS = 3
ROPE_THETA = 10000.0
RMS_EPS = 1e-6
D_FF_DENSE = 2816
N_EXPERTS = 8
TOP_K = 2
D_FF_EXPERT = 3584

WIDTH_A = MOBA_HEADS * HEAD_DIM
WIDTH_B = N_DIL_GROUPS * DIL_HEADS_PER_GROUP * HEAD_DIM
WIDTH_B_OUT = DIL_HEADS_PER_GROUP * HEAD_DIM
IN_PROJ_COLS = 3 * WIDTH_A + 3 * WIDTH_B + 2 * D_MODEL

kernel_name = 'hybrid_moba_dilated_gated_moe'


def rms_norm(x, g):
    xf = x.astype(jnp.float32)
    y = xf * lax.rsqrt(jnp.mean(xf * xf, axis=-1, keepdims=True) + RMS_EPS)
    return (y * g.astype(jnp.float32)).astype(x.dtype)


def rope(x, pos):
    half = x.shape[-1] // 2
    inv = ROPE_THETA ** (-jnp.arange(half, dtype=jnp.float32) / half)
    ang = pos.astype(jnp.float32)[:, None] * inv[None, :]
    cos = jnp.cos(ang)[None, :, None, :]
    sin = jnp.sin(ang)[None, :, None, :]
    x1 = x[..., :half].astype(jnp.float32)
    x2 = x[..., half:].astype(jnp.float32)
    return jnp.concatenate([x1 * cos - x2 * sin, x2 * cos + x1 * sin], axis=-1).astype(x.dtype)


def moba_attention(q, k, v):
    bsz, seq, n_heads, head_dim = q.shape
    n_blk = -(-seq // MOBA_BLOCK)
    seq_p = n_blk * MOBA_BLOCK
    pad = ((0, 0), (0, seq_p - seq), (0, 0), (0, 0))
    qh = jnp.pad(q, pad).transpose(0, 2, 1, 3)
    k_blk = jnp.pad(k, pad).transpose(0, 2, 1, 3).reshape(bsz, n_heads, n_blk, MOBA_BLOCK, head_dim)
    v_blk = jnp.pad(v, pad).transpose(0, 2, 1, 3).reshape(bsz, n_heads, n_blk, MOBA_BLOCK, head_dim)
    scale = head_dim ** -0.5
    q_blk_id = jnp.arange(seq_p) // MOBA_BLOCK
    n_sel = min(MOBA_TOPK, n_blk - 1)
    if n_sel > 0:
        k_mean = jnp.mean(k_blk.astype(jnp.float32), axis=3)
        gate = jnp.einsum('bhsd,bhnd->bhsn', qh.astype(jnp.float32), k_mean)
        fully_past = jnp.arange(n_blk)[None, :] < q_blk_id[:, None]
        gate = jnp.where(fully_past, gate, -jnp.inf)
        _, sel = lax.top_k(gate, n_sel)
        sel_ok = sel < q_blk_id[:, None]
    gather = jax.vmap(jax.vmap(lambda kb, ix: kb[ix]))
    local = jnp.arange(MOBA_BLOCK)

    def chunk(c):
        start = c * MOBA_Q_CHUNK
        blk = start // MOBA_BLOCK
        qc = lax.dynamic_slice_in_dim(qh, start, MOBA_Q_CHUNK, axis=2)
        k_own = lax.dynamic_index_in_dim(k_blk, blk, axis=2, keepdims=False)
        v_own = lax.dynamic_index_in_dim(v_blk, blk, axis=2, keepdims=False)
        q_pos = start + jnp.arange(MOBA_Q_CHUNK)
        k_pos = blk * MOBA_BLOCK + local
        s_own = jnp.einsum('bhqd,bhkd->bhqk', qc, k_own).astype(jnp.float32) * scale
        s_own = jnp.where(k_pos[None, :] <= q_pos[:, None], s_own, -jnp.inf)
        if n_sel == 0:
            p = jax.nn.softmax(s_own, axis=-1).astype(v.dtype)
            return jnp.einsum('bhqk,bhkd->bhqd', p, v_own)
        sel_c = lax.dynamic_slice_in_dim(sel, start, MOBA_Q_CHUNK, axis=2)
        ok_c = lax.dynamic_slice_in_dim(sel_ok, start, MOBA_Q_CHUNK, axis=2)
        k_sel = gather(k_blk, sel_c)
        v_sel = gather(v_blk, sel_c)
        s_sel = jnp.einsum('bhqd,bhqnkd->bhqnk', qc, k_sel).astype(jnp.float32) * scale
        s_sel = jnp.where(ok_c[..., None], s_sel, -jnp.inf)
        n_past = n_sel * MOBA_BLOCK
        s = jnp.concatenate([s_sel.reshape(bsz, n_heads, MOBA_Q_CHUNK, n_past), s_own], axis=-1)
        p = jax.nn.softmax(s, axis=-1).astype(v.dtype)
        p_sel = p[..., :n_past].reshape(bsz, n_heads, MOBA_Q_CHUNK, n_sel, MOBA_BLOCK)
        return (jnp.einsum('bhqnk,bhqnkd->bhqd', p_sel, v_sel)
                + jnp.einsum('bhqk,bhkd->bhqd', p[..., n_past:], v_own))

    out = lax.map(chunk, jnp.arange(seq_p // MOBA_Q_CHUNK))
    out = out.transpose(1, 0, 3, 2, 4).reshape(bsz, seq_p, n_heads, head_dim)
    return out[:, :seq]


def dilated_window_attention(q, k, v, dilation, steps):
    bsz, seq, n_heads, head_dim = q.shape
    sub = seq // dilation

    def to_residue(t):
        return t.reshape(bsz, sub, dilation, n_heads, head_dim).transpose(0, 2, 1, 3, 4).reshape(
            bsz * dilation, sub, n_heads, head_dim)

    def from_residue(t):
        tail = t.shape[2:]
        return t.reshape((bsz, dilation, sub) + tail).transpose((0, 2, 1) + tuple(range(3, 3 + len(tail)))).reshape(
            (bsz, seq) + tail)

    qr, kr, vr = to_residue(q), to_residue(k), to_residue(v)
    rows = bsz * dilation
    n_blk = -(-sub // steps)
    sub_p = n_blk * steps
    qb = jnp.pad(qr, ((0, 0), (0, sub_p - sub), (0, 0), (0, 0))).reshape(rows, n_blk, steps, n_heads, head_dim)
    kv_pad = ((0, 0), (steps, sub_p - sub), (0, 0), (0, 0))
    kp = jnp.pad(kr, kv_pad)
    vp = jnp.pad(vr, kv_pad)

    def band(t):
        return jnp.concatenate([t[:, :sub_p].reshape(rows, n_blk, steps, n_heads, head_dim),
                                t[:, steps:].reshape(rows, n_blk, steps, n_heads, head_dim)], axis=2)

    kb, vb = band(kp), band(vp)
    s = jnp.einsum('nbqhd,nbkhd->nbhqk', qb, kb).astype(jnp.float32) * (head_dim ** -0.5)
    qi = jnp.arange(steps)[:, None]
    kj = jnp.arange(2 * steps)[None, :]
    key_idx = jnp.arange(n_blk)[:, None, None] * steps - steps + kj
    mask = (kj >= qi) & (kj <= qi + steps) & (key_idx >= 0)
    s = jnp.where(mask[None, :, None], s, -jnp.inf)
    lse = jax.nn.logsumexp(s, axis=-1)
    p = jnp.exp(s - lse[..., None]).astype(v.dtype)
    o = jnp.einsum('nbhqk,nbkhd->nbqhd', p, vb).reshape(rows, sub_p, n_heads, head_dim)[:, :sub]
    lse = lse.transpose(0, 1, 3, 2).reshape(rows, sub_p, n_heads)[:, :sub]
    return from_residue(o), from_residue(lse)


def dilated_mixture(q, k, v):
    outs, lses = [], []
    for g, (window, dilation) in enumerate(DIL_PATTERNS):
        sl = slice(g * DIL_HEADS_PER_GROUP, (g + 1) * DIL_HEADS_PER_GROUP)
        o, l = dilated_window_attention(q[:, :, sl], k[:, :, sl], v[:, :, sl], dilation, window // dilation)
        outs.append(o)
        lses.append(l)
    w = jax.nn.softmax(jnp.stack(lses, axis=0), axis=0).astype(q.dtype)
    return jnp.sum(w[..., None] * jnp.stack(outs, axis=0), axis=0)


def mixer_block(h, w_in, w_proj_a, w_proj_b, w_out):
    bsz, seq, _ = h.shape
    proj = jnp.einsum('bsd,dc->bsc', h, w_in)
    sizes = [WIDTH_A] * 3 + [WIDTH_B] * 3 + [D_MODEL] * 2
    qa, ka, va, qb, kb, vb, ga, gb = jnp.split(proj, np.cumsum(sizes)[:-1].tolist(), axis=-1)
    pos = jnp.arange(seq)
    heads_a = (bsz, seq, MOBA_HEADS, HEAD_DIM)
    heads_b = (bsz, seq, N_DIL_GROUPS * DIL_HEADS_PER_GROUP, HEAD_DIM)
    oa = moba_attention(rope(qa.reshape(heads_a), pos), rope(ka.reshape(heads_a), pos), va.reshape(heads_a))
    ob = dilated_mixture(rope(qb.reshape(heads_b), pos), rope(kb.reshape(heads_b), pos), vb.reshape(heads_b))
    ya = jnp.einsum('bsc,cd->bsd', oa.reshape(bsz, seq, WIDTH_A), w_proj_a)
    yb = jnp.einsum('bsc,cd->bsd', ob.reshape(bsz, seq, WIDTH_B_OUT), w_proj_b)
    merged = jax.nn.sigmoid(ga) * ya + jax.nn.sigmoid(gb) * yb
    return jnp.einsum('bsd,de->bse', merged, w_out)


def swiglu(h, w_gate, w_up, w_down):
    a = jnp.einsum('bsd,df->bsf', h, w_gate)
    b = jnp.einsum('bsd,df->bsf', h, w_up)
    return jnp.einsum('bsf,fd->bsd', jax.nn.silu(a) * b, w_down)


def moe_swiglu(h, w_router, w_gate_e, w_up_e, w_down_e):
    logits = jnp.einsum('bsd,de->bse', h, w_router).astype(jnp.float32)
    top_val, top_idx = lax.top_k(logits, TOP_K)
    top_w = jax.nn.softmax(top_val, axis=-1)
    combine = jnp.sum(jax.nn.one_hot(top_idx, N_EXPERTS, dtype=jnp.float32) * top_w[..., None], axis=-2)
    combine = combine.astype(h.dtype)
    y = jnp.zeros_like(h)
    for e in range(N_EXPERTS):
        y = y + combine[..., e:e + 1] * swiglu(h, w_gate_e[e], w_up_e[e], w_down_e[e])
    return y


def setup_inputs(seed: int = 0) -> dict:
    key = jax.random.key(seed)
    keys = iter(jax.random.split(key, 32))

    def dense(shape, fan_in):
        return jax.random.normal(next(keys), shape, jnp.float32) * (fan_in ** -0.5)

    def gain():
        return 1.0 + 0.02 * jax.random.normal(next(keys), (D_MODEL,), jnp.float32)

    x = jax.random.normal(next(keys), (BATCH, SEQ, D_MODEL), jnp.float32)
    return {
        'x': x,
        'norm_mix_0': gain(),
        'w_in_0': dense((D_MODEL, IN_PROJ_COLS), D_MODEL),
        'w_proj_a_0': dense((WIDTH_A, D_MODEL), WIDTH_A),
        'w_proj_b_0': dense((WIDTH_B_OUT, D_MODEL), WIDTH_B_OUT),
        'w_out_0': dense((D_MODEL, D_MODEL), D_MODEL),
        'norm_ffn_0': gain(),
        'w_gate_0': dense((D_MODEL, D_FF_DENSE), D_MODEL),
        'w_up_0': dense((D_MODEL, D_FF_DENSE), D_MODEL),
        'w_down_0': dense((D_FF_DENSE, D_MODEL), D_FF_DENSE),
        'norm_mix_1': gain(),
        'w_in_1': dense((D_MODEL, IN_PROJ_COLS), D_MODEL),
        'w_proj_a_1': dense((WIDTH_A, D_MODEL), WIDTH_A),
        'w_proj_b_1': dense((WIDTH_B_OUT, D_MODEL), WIDTH_B_OUT),
        'w_out_1': dense((D_MODEL, D_MODEL), D_MODEL),
        'norm_ffn_1': gain(),
        'w_router_1': dense((D_MODEL, N_EXPERTS), D_MODEL),
        'w_gate_e_1': dense((N_EXPERTS, D_MODEL, D_FF_EXPERT), D_MODEL),
        'w_up_e_1': dense((N_EXPERTS, D_MODEL, D_FF_EXPERT), D_MODEL),
        'w_down_e_1': dense((N_EXPERTS, D_FF_EXPERT, D_MODEL), D_FF_EXPERT),
        'norm_final': gain(),
    }


def reference(x, norm_mix_0, w_in_0, w_proj_a_0, w_proj_b_0, w_out_0, norm_ffn_0, w_gate_0, w_up_0, w_down_0,
              norm_mix_1, w_in_1, w_proj_a_1, w_proj_b_1, w_out_1, norm_ffn_1, w_router_1, w_gate_e_1, w_up_e_1,
              w_down_e_1, norm_final):
    mixers = ((norm_mix_0, w_in_0, w_proj_a_0, w_proj_b_0, w_out_0),
              (norm_mix_1, w_in_1, w_proj_a_1, w_proj_b_1, w_out_1))
    ffns = ((norm_ffn_0, (w_gate_0, w_up_0, w_down_0)),
            (norm_ffn_1, (w_router_1, w_gate_e_1, w_up_e_1, w_down_e_1)))
    for layer in range(DEPTH):
        nm, wi, wa, wb, wo = mixers[layer]
        x = x + mixer_block(rms_norm(x, nm), wi, wa, wb, wo)
        nf, fp = ffns[layer]
        h = rms_norm(x, nf)
        if layer % 2 == 0:
            x = x + swiglu(h, *fp)
        else:
            x = x + moe_swiglu(h, *fp)
    return rms_norm(x, norm_final)
```

```python
import functools

import jax
import jax.numpy as jnp
import numpy as np
from jax import lax
from jax.experimental import pallas as pl
from jax.experimental.pallas import tpu as pltpu

D_MODEL = 1024
HEAD_DIM = 64
MOBA_HEADS = 8
MOBA_BLOCK = 256
MOBA_TOPK = 3
DIL_DILATIONS = (1, 4, 16)
DIL_STEPS = 128
DIL_HEADS_PER_GROUP = 4
ROPE_THETA = 10000.0
RMS_EPS = 1e-6
N_EXPERTS = 8

WIDTH_A = MOBA_HEADS * HEAD_DIM
WIDTH_BG = DIL_HEADS_PER_GROUP * HEAD_DIM
WIDTH_B = len(DIL_DILATIONS) * WIDTH_BG

LANES = 128
NEG = -1e30
VMEM_LIMIT = 56 * 1024 * 1024

COL_GA = 0
COL_GB = COL_GA + D_MODEL
COL_QA = COL_GB + D_MODEL
COL_KA = COL_QA + WIDTH_A
COL_VA = COL_KA + WIDTH_A
COL_QB = COL_VA + WIDTH_A
COL_KB = COL_QB + WIDTH_B
COL_VB = COL_KB + WIDTH_B
PROJ_COLS = COL_VB + WIDTH_B

F32 = jnp.float32
BF16 = jnp.bfloat16


def _params(*sem):
    return pltpu.CompilerParams(dimension_semantics=sem, vmem_limit_bytes=VMEM_LIMIT)


def _dot(a, b):
    return jnp.dot(a, b, preferred_element_type=F32)


def _dot_nt(a, b):
    return lax.dot_general(a, b, (((1,), (1,)), ((), ())), preferred_element_type=F32)


def _rms_norm(x, g):
    return x * lax.rsqrt(jnp.mean(x * x, axis=-1, keepdims=True) + RMS_EPS) * g


def _sigmoid(z):
    return 1.0 / (1.0 + jnp.exp(-z))


def _rope_apply(acc, cos, sin_signed):
    width = acc.shape[-1]
    lane = lax.broadcasted_iota(jnp.int32, acc.shape, 1)
    first_half = (lane % HEAD_DIM) < (HEAD_DIM // 2)
    rot = jnp.where(first_half,
                    pltpu.roll(acc, width - HEAD_DIM // 2, 1),
                    pltpu.roll(acc, HEAD_DIM // 2, 1))
    return acc * cos + rot * sin_signed


def _in_proj_kernel(x_ref, g_ref, w_ref, cos_ref, sin_ref, o_ref):
    h = _rms_norm(x_ref[...], g_ref[...]).astype(BF16)
    plain = ((COL_GA, D_MODEL), (COL_GB, D_MODEL), (COL_VA, WIDTH_A), (COL_VB, WIDTH_B))
    rotary = ((COL_QA, WIDTH_A), (COL_KA, WIDTH_A), (COL_QB, WIDTH_B), (COL_KB, WIDTH_B))
    for c0, w in plain:
        o_ref[:, c0:c0 + w] = _dot(h, w_ref[:, c0:c0 + w]).astype(BF16)
    for c0, w in rotary:
        acc = _dot(h, w_ref[:, c0:c0 + w])
        o_ref[:, c0:c0 + w] = _rope_apply(acc, cos_ref[:, :w], sin_ref[:, :w]).astype(BF16)


def _in_proj(x, g, w, cos, sin, tm):
    n = x.shape[0]
    seq = cos.shape[0]
    n_seq_tiles = seq // tm
    return pl.pallas_call(
        _in_proj_kernel,
        grid=(n // tm,),
        in_specs=[
            pl.BlockSpec((tm, D_MODEL), lambda i: (i, 0)),
            pl.BlockSpec((1, D_MODEL), lambda i: (0, 0)),
            pl.BlockSpec((D_MODEL, PROJ_COLS), lambda i: (0, 0), pipeline_mode=pl.Buffered(1)),
            pl.BlockSpec((tm, WIDTH_B), lambda i: (i % n_seq_tiles, 0)),
            pl.BlockSpec((tm, WIDTH_B), lambda i: (i % n_seq_tiles, 0)),
        ],
        out_specs=pl.BlockSpec((tm, PROJ_COLS), lambda i: (i, 0)),
        out_shape=jax.ShapeDtypeStruct((n, PROJ_COLS), BF16),
        compiler_params=_params("parallel"),
        name="in_proj",
    )(x, g, w, cos, sin)


def _moba_kernel(q_ref, k_ref, v_ref, o_ref, qa_ref, qb_ref, ka_ref, kb_ref, *, seq):
    n_blk = seq // MOBA_BLOCK
    q2 = q_ref[...]
    k2 = k_ref[...]
    lane = lax.broadcasted_iota(jnp.int32, (seq, LANES), 1)
    row = lax.broadcasted_iota(jnp.int32, (seq, LANES), 0)
    is_a = lane < HEAD_DIM

    kmean = jnp.mean(k2.astype(F32).reshape(n_blk, MOBA_BLOCK, LANES), axis=1)
    lane8 = lax.broadcasted_iota(jnp.int32, (n_blk, LANES), 1)
    km = jnp.concatenate([jnp.where(lane8 < HEAD_DIM, kmean, 0.0),
                          jnp.where(lane8 >= HEAD_DIM, kmean, 0.0)], axis=0)
    km_hi = km.astype(BF16)
    km_lo = (km - km_hi.astype(F32)).astype(BF16)
    g_all = _dot_nt(jnp.concatenate([km_hi, km_lo], axis=0), q2)
    gates = (g_all[0:n_blk] + g_all[2 * n_blk:3 * n_blk],
             g_all[n_blk:2 * n_blk] + g_all[3 * n_blk:4 * n_blk])

    blk = lax.broadcasted_iota(jnp.int32, (n_blk, seq), 0)
    q_blk = lax.broadcasted_iota(jnp.int32, (n_blk, seq), 1) // MOBA_BLOCK
    past = blk < q_blk
    bias_t = []
    for g in gates:
        g = jnp.where(past, g, -jnp.inf)
        rank = jnp.zeros((n_blk, seq), jnp.int32)
        for jp in range(n_blk):
            gj = g[jp:jp + 1, :]
            beats = (gj > g) | ((gj == g) & (jp < blk))
            rank = rank + beats.astype(jnp.int32)
        keep = ((rank < MOBA_TOPK) & past) | (blk == q_blk)
        bias_t.append(jnp.where(keep, 0.0, NEG))
    pad = jnp.zeros((HEAD_DIM - n_blk, seq), F32)
    bias = jnp.concatenate([bias_t[1], pad, bias_t[0], pad], axis=0).T.astype(BF16)
    key_blk = row // MOBA_BLOCK
    onehot = ((lane == key_blk) | (lane == key_blk + HEAD_DIM)).astype(BF16)

    qa_ref[...] = jnp.where(is_a, q2, bias)
    qb_ref[...] = jnp.where(is_a, bias, q2)
    ka_ref[...] = jnp.where(is_a, k2, onehot)
    kb_ref[...] = jnp.where(is_a, onehot, k2)

    tri_r = lax.broadcasted_iota(jnp.int32, (MOBA_BLOCK, MOBA_BLOCK), 0)
    tri_c = lax.broadcasted_iota(jnp.int32, (MOBA_BLOCK, MOBA_BLOCK), 1)
    causal = tri_c <= tri_r
    out_lane = lax.broadcasted_iota(jnp.int32, (MOBA_BLOCK, LANES), 1) < HEAD_DIM

    def head(qx_ref, kx_ref, i):
        rows_i = pl.ds(pl.multiple_of(i * MOBA_BLOCK, MOBA_BLOCK), MOBA_BLOCK)
        qx = qx_ref[rows_i, :]
        s = jnp.where(causal, _dot_nt(qx, kx_ref[rows_i, :]), NEG)
        m = jnp.max(s, axis=1, keepdims=True)
        p = jnp.exp(s - m)
        l = jnp.sum(p, axis=1, keepdims=True)
        acc = _dot(p.astype(BF16), v_ref[rows_i, :])

        def past_block(j, carry):
            m, l, acc = carry
            rows_j = pl.ds(pl.multiple_of(j * MOBA_BLOCK, MOBA_BLOCK), MOBA_BLOCK)
            s = _dot_nt(qx, kx_ref[rows_j, :])
            m_new = jnp.maximum(m, jnp.max(s, axis=1, keepdims=True))
            alpha = jnp.exp(m - m_new)
            p = jnp.exp(s - m_new)
            l = alpha * l + jnp.sum(p, axis=1, keepdims=True)
            acc = alpha * acc + _dot(p.astype(BF16), v_ref[rows_j, :])
            return m_new, l, acc

        m, l, acc = lax.fori_loop(0, i, past_block, (m, l, acc))
        return acc * (1.0 / l)

    def q_block(i, _):
        rows_i = pl.ds(pl.multiple_of(i * MOBA_BLOCK, MOBA_BLOCK), MOBA_BLOCK)
        out = jnp.where(out_lane, head(qa_ref, ka_ref, i), head(qb_ref, kb_ref, i))
        o_ref[rows_i, :] = out.astype(o_ref.dtype)
        return 0

    lax.fori_loop(0, n_blk, q_block, 0)


def _moba(proj, bsz, seq):
    n = bsz * seq
    qc, kc, vc = COL_QA // LANES, COL_KA // LANES, COL_VA // LANES
    blk = (seq, LANES)
    return pl.pallas_call(
        functools.partial(_moba_kernel, seq=seq),
        grid=(bsz, WIDTH_A // LANES),
        in_specs=[
            pl.BlockSpec(blk, lambda b, h: (b, qc + h)),
            pl.BlockSpec(blk, lambda b, h: (b, kc + h)),
            pl.BlockSpec(blk, lambda b, h: (b, vc + h)),
        ],
        out_specs=pl.BlockSpec(blk, lambda b, h: (b, h)),
        out_shape=jax.ShapeDtypeStruct((n, WIDTH_A), BF16),
        scratch_shapes=[pltpu.VMEM(blk, BF16) for _ in range(4)],
        compiler_params=_params("parallel", "parallel"),
        name="moba",
    )(proj, proj, proj)


def _dilated_kernel(*refs, seq):
    n_groups = len(DIL_DILATIONS)
    in_refs = refs[:3 * n_groups]
    o_ref = refs[3 * n_groups]
    slab_ref, qres_ref, kres_ref, vres_ref, on_ref, ln_ref = refs[3 * n_groups + 1:]
    steps = DIL_STEPS
    n_pairs = WIDTH_BG // LANES
    n_row_blocks = seq // steps

    zeros_pad = jnp.zeros((steps, WIDTH_BG), BF16)
    kres_ref[0:steps, :] = zeros_pad
    vres_ref[0:steps, :] = zeros_pad

    qi = lax.broadcasted_iota(jnp.int32, (steps, 2 * steps), 0)
    kj = lax.broadcasted_iota(jnp.int32, (steps, 2 * steps), 1)
    band = (kj >= qi) & (kj <= qi + steps)
    cur_half = kj >= steps
    is_a = lax.broadcasted_iota(jnp.int32, (steps, LANES), 1) < HEAD_DIM

    for g, d in enumerate(DIL_DILATIONS):
        sub = seq // d
        n_blk = sub // steps
        q_ref, k_ref, v_ref = in_refs[3 * g:3 * g + 3]
        for src, dst, off in ((q_ref, qres_ref, 0), (k_ref, kres_ref, steps), (v_ref, vres_ref, steps)):
            if d == 1:
                dst[off:off + seq, :] = src[...]
                continue
            for hp in range(n_pairs):
                cols = slice(hp * LANES, (hp + 1) * LANES)
                slab_ref[...] = src[:, cols].astype(F32)
                for r in range(d):
                    dst[off + r * sub:off + (r + 1) * sub, cols] = (
                        slab_ref[pl.ds(r, sub, stride=d), :].astype(BF16))

        def row_block(c, _, d=d, n_blk=n_blk, g=g):
            rows_q = pl.ds(pl.multiple_of(c * steps, steps), steps)
            rows_kv = pl.ds(pl.multiple_of(c * steps, steps), 2 * steps)
            prev_valid = (c % n_blk) != 0
            mask = band & (cur_half | prev_valid)
            start = (c % n_blk) * (steps * d) + c // n_blk
            for hp in range(n_pairs):
                cols = slice(hp * LANES, (hp + 1) * LANES)
                q2 = qres_ref[rows_q, cols]
                k2 = kres_ref[rows_kv, cols]
                v2 = vres_ref[rows_kv, cols]
                outs, lses = [], []
                for head_a in (True, False):
                    qm = jnp.where(is_a if head_a else ~is_a, q2, jnp.zeros_like(q2))
                    s = jnp.where(mask, _dot_nt(qm, k2), NEG)
                    m = jnp.max(s, axis=1, keepdims=True)
                    p = jnp.exp(s - m)
                    l = jnp.sum(p, axis=1, keepdims=True)
                    outs.append(_dot(p.astype(BF16), v2) * (1.0 / l))
                    lses.append(jnp.broadcast_to(m + jnp.log(l), (steps, LANES)))
                o2 = jnp.where(is_a, outs[0], outs[1])
                l2 = jnp.where(is_a, lses[0], lses[1])
                slot = g * n_pairs + hp
                if d == 1:
                    on_ref[slot, rows_q, :] = o2
                    ln_ref[slot, rows_q, :] = l2
                else:
                    on_ref[slot, pl.ds(start, steps, stride=d), :] = o2
                    ln_ref[slot, pl.ds(start, steps, stride=d), :] = l2
            return 0

        lax.fori_loop(0, n_row_blocks, row_block, 0)

    for hp in range(n_pairs):
        ls = [ln_ref[g * n_pairs + hp] for g in range(n_groups)]
        mx = functools.reduce(jnp.maximum, ls)
        es = [jnp.exp(l - mx) for l in ls]
        inv = 1.0 / functools.reduce(lambda a, b: a + b, es)
        out = functools.reduce(lambda a, b: a + b,
                               [(es[g] * inv) * on_ref[g * n_pairs + hp] for g in range(n_groups)])
        o_ref[:, hp * LANES:(hp + 1) * LANES] = out.astype(o_ref.dtype)


def _dilated(proj, bsz, seq):
    n = bsz * seq
    n_groups = len(DIL_DILATIONS)
    blk = (seq, WIDTH_BG)
    in_specs = []
    for g in range(n_groups):
        for col in (COL_QB, COL_KB, COL_VB):
            cb = col // WIDTH_BG + g
            in_specs.append(pl.BlockSpec(blk, lambda b, cb=cb: (b, cb)))
    n_slots = n_groups * (WIDTH_BG // LANES)
    return pl.pallas_call(
        functools.partial(_dilated_kernel, seq=seq),
        grid=(bsz,),
        in_specs=in_specs,
        out_specs=pl.BlockSpec(blk, lambda b: (b, 0)),
        out_shape=jax.ShapeDtypeStruct((n, WIDTH_BG), BF16),
        scratch_shapes=[
            pltpu.VMEM((seq, LANES), F32),
            pltpu.VMEM((seq, WIDTH_BG), BF16),
            pltpu.VMEM((seq + DIL_STEPS, WIDTH_BG), BF16),
            pltpu.VMEM((seq + DIL_STEPS, WIDTH_BG), BF16),
            pltpu.VMEM((n_slots, seq, LANES), F32),
            pltpu.VMEM((n_slots, seq, LANES), F32),
        ],
        compiler_params=_params("parallel"),
        name="dilated",
    )(*([proj] * (3 * n_groups)))


def _merge_kernel(x_ref, oa_ref, ob_ref, ga_ref, gb_ref, wa_ref, wb_ref, wo_ref, o_ref):
    ya = _dot(oa_ref[...], wa_ref[...])
    yb = _dot(ob_ref[...], wb_ref[...])
    merged = _sigmoid(ga_ref[...].astype(F32)) * ya + _sigmoid(gb_ref[...].astype(F32)) * yb
    o_ref[...] = x_ref[...] + _dot(merged.astype(BF16), wo_ref[...])


def _merge(x, oa, ob, proj, wa, wb, wo, tm):
    n = x.shape[0]
    const = lambda i: (0, 0)
    return pl.pallas_call(
        _merge_kernel,
        grid=(n // tm,),
        in_specs=[
            pl.BlockSpec((tm, D_MODEL), lambda i: (i, 0)),
            pl.BlockSpec((tm, WIDTH_A), lambda i: (i, 0)),
            pl.BlockSpec((tm, WIDTH_BG), lambda i: (i, 0)),
            pl.BlockSpec((tm, D_MODEL), lambda i: (i, COL_GA // D_MODEL)),
            pl.BlockSpec((tm, D_MODEL), lambda i: (i, COL_GB // D_MODEL)),
            pl.BlockSpec((WIDTH_A, D_MODEL), const),
            pl.BlockSpec((WIDTH_BG, D_MODEL), const),
            pl.BlockSpec((D_MODEL, D_MODEL), const),
        ],
        out_specs=pl.BlockSpec((tm, D_MODEL), lambda i: (i, 0)),
        out_shape=jax.ShapeDtypeStruct((n, D_MODEL), F32),
        compiler_params=_params("parallel"),
        name="merge",
    )(x, oa, ob, proj, proj, wa, wb, wo)


def _swiglu_kernel(x_ref, g_ref, wg_ref, wu_ref, wd_ref, o_ref, h_ref, acc_ref):
    f = pl.program_id(1)

    @pl.when(f == 0)
    def _():
        h_ref[...] = _rms_norm(x_ref[...], g_ref[...]).astype(BF16)
        acc_ref[...] = jnp.zeros_like(acc_ref)

    h = h_ref[...]
    a = _dot(h, wg_ref[...])
    b = _dot(h, wu_ref[...])
    t = (a * _sigmoid(a) * b).astype(BF16)
    acc_ref[...] += _dot(t, wd_ref[...])

    @pl.when(f == pl.num_programs(1) - 1)
    def _():
        o_ref[...] = x_ref[...] + acc_ref[...]


def _swiglu(x, g, wg, wu, wd, tm, tf):
    n = x.shape[0]
    d_ff = wg.shape[1]
    return pl.pallas_call(
        _swiglu_kernel,
        grid=(n // tm, d_ff // tf),
        in_specs=[
            pl.BlockSpec((tm, D_MODEL), lambda i, f: (i, 0)),
            pl.BlockSpec((1, D_MODEL), lambda i, f: (0, 0)),
            pl.BlockSpec((D_MODEL, tf), lambda i, f: (0, f)),
            pl.BlockSpec((D_MODEL, tf), lambda i, f: (0, f)),
            pl.BlockSpec((tf, D_MODEL), lambda i, f: (f, 0)),
        ],
        out_specs=pl.BlockSpec((tm, D_MODEL), lambda i, f: (i, 0)),
        out_shape=jax.ShapeDtypeStruct((n, D_MODEL), F32),
        scratch_shapes=[pltpu.VMEM((tm, D_MODEL), BF16), pltpu.VMEM((tm, D_MODEL), F32)],
        compiler_params=_params("parallel", "arbitrary"),
        name="swiglu",
    )(x, g, wg, wu, wd)


def _router_combine(h, wr):
    logits = jnp.dot(h, wr, preferred_element_type=F32, precision=lax.Precision.HIGHEST)
    lane = lax.broadcasted_iota(jnp.int32, logits.shape, 1)
    logits = jnp.where(lane < N_EXPERTS, logits, -jnp.inf)
    v1 = jnp.max(logits, axis=1, keepdims=True)
    i1 = jnp.min(jnp.where(logits == v1, lane, LANES), axis=1, keepdims=True)
    rest = jnp.where(lane == i1, -jnp.inf, logits)
    v2 = jnp.max(rest, axis=1, keepdims=True)
    i2 = jnp.min(jnp.where(rest == v2, lane, LANES), axis=1, keepdims=True)
    e2 = jnp.exp(v2 - v1)
    w1 = 1.0 / (1.0 + e2)
    w2 = e2 / (1.0 + e2)
    return jnp.where(lane == i1, w1, 0.0) + jnp.where(lane == i2, w2, 0.0)


def _moe_kernel(x_ref, g_ref, wr_ref, wg_ref, wu_ref, wd_ref, gf_ref, o_ref, h_ref, acc_ref, comb_ref):
    e = pl.program_id(1)
    f = pl.program_id(2)
    first = (e == 0) & (f == 0)
    last = (e == pl.num_programs(1) - 1) & (f == pl.num_programs(2) - 1)

    @pl.when(first)
    def _():
        h = _rms_norm(x_ref[...], g_ref[...])
        h_ref[...] = h.astype(BF16)
        comb_ref[...] = _router_combine(h, wr_ref[...])
        acc_ref[...] = jnp.zeros_like(acc_ref)

    h = h_ref[...]
    a = _dot(h, wg_ref[0])
    b = _dot(h, wu_ref[0])
    comb = comb_ref[...]
    lane = lax.broadcasted_iota(jnp.int32, comb.shape, 1)
    c_e = jnp.sum(jnp.where(lane == e, comb, 0.0), axis=1, keepdims=True)
    t = (a * _sigmoid(a) * b * c_e).astype(BF16)
    acc_ref[...] += _dot(t, wd_ref[0])

    @pl.when(last)
    def _():
        o_ref[...] = _rms_norm(x_ref[...] + acc_ref[...], gf_ref[...])


def _moe(x, g, wr, wg, wu, wd, g_final, tm, tf):
    n = x.shape[0]
    n_exp, _, d_ff = wg.shape
    return pl.pallas_call(
        _moe_kernel,
        grid=(n // tm, n_exp, d_ff // tf),
        in_specs=[
            pl.BlockSpec((tm, D_MODEL), lambda i, e, f: (i, 0)),
            pl.BlockSpec((1, D_MODEL), lambda i, e, f: (0, 0)),
            pl.BlockSpec((D_MODEL, LANES), lambda i, e, f: (0, 0)),
            pl.BlockSpec((1, D_MODEL, tf), lambda i, e, f: (e, 0, f)),
            pl.BlockSpec((1, D_MODEL, tf), lambda i, e, f: (e, 0, f)),
            pl.BlockSpec((1, tf, D_MODEL), lambda i, e, f: (e, f, 0)),
            pl.BlockSpec((1, D_MODEL), lambda i, e, f: (0, 0)),
        ],
        out_specs=pl.BlockSpec((tm, D_MODEL), lambda i, e, f: (i, 0)),
        out_shape=jax.ShapeDtypeStruct((n, D_MODEL), F32),
        scratch_shapes=[pltpu.VMEM((tm, D_MODEL), BF16), pltpu.VMEM((tm, D_MODEL), F32),
                        pltpu.VMEM((tm, LANES), F32)],
        compiler_params=_params("parallel", "arbitrary", "arbitrary"),
        name="moe",
    )(x, g, wr, wg, wu, wd, g_final)


def _prep_w_in(w_in):
    sizes = [WIDTH_A] * 3 + [WIDTH_B] * 3 + [D_MODEL] * 2
    qa, ka, va, qb, kb, vb, ga, gb = jnp.split(w_in, np.cumsum(sizes)[:-1].tolist(), axis=1)
    scale = HEAD_DIM ** -0.5
    return jnp.concatenate([ga, gb, qa * scale, ka, va, qb * scale, kb, vb], axis=1).astype(BF16)


def _rope_tables(seq):
    half = HEAD_DIM // 2
    inv = ROPE_THETA ** (-jnp.arange(half, dtype=F32) / half)
    ang = jnp.arange(seq, dtype=F32)[:, None] * inv[None, :]
    cos, sin = jnp.cos(ang), jnp.sin(ang)
    n_heads = WIDTH_B // HEAD_DIM
    return (jnp.tile(jnp.concatenate([cos, cos], axis=1), (1, n_heads)),
            jnp.tile(jnp.concatenate([-sin, sin], axis=1), (1, n_heads)))


def kernel(x, norm_mix_0, w_in_0, w_proj_a_0, w_proj_b_0, w_out_0, norm_ffn_0, w_gate_0, w_up_0, w_down_0,
           norm_mix_1, w_in_1, w_proj_a_1, w_proj_b_1, w_out_1, norm_ffn_1, w_router_1, w_gate_e_1, w_up_e_1,
           w_down_e_1, norm_final):
    bsz, seq, _ = x.shape
    assert seq % MOBA_BLOCK == 0 and seq % (DIL_STEPS * max(DIL_DILATIONS)) == 0
    cos, sin = _rope_tables(seq)
    row = lambda v: v.reshape(1, D_MODEL).astype(F32)
    xf = x.reshape(bsz * seq, D_MODEL)

    def mixer(xf, nm, w_in, wa, wb, wo):
        proj = _in_proj(xf, row(nm), _prep_w_in(w_in), cos, sin, tm=512)
        oa = _moba(proj, bsz, seq)
        ob = _dilated(proj, bsz, seq)
        return _merge(xf, oa, ob, proj, wa.astype(BF16), wb.astype(BF16), wo.astype(BF16), tm=512)

    xf = mixer(xf, norm_mix_0, w_in_0, w_proj_a_0, w_proj_b_0, w_out_0)
    xf = _swiglu(xf, row(norm_ffn_0), w_gate_0.astype(BF16), w_up_0.astype(BF16), w_down_0.astype(BF16),
                 tm=1024, tf=256)
    xf = mixer(xf, norm_mix_1, w_in_1, w_proj_a_1, w_proj_b_1, w_out_1)
    wr = jnp.pad(w_router_1.astype(F32), ((0, 0), (0, LANES - N_EXPERTS)))
    xf = _moe(xf, row(norm_ffn_1), wr, w_gate_e_1.astype(BF16), w_up_e_1.astype(BF16),
              w_down_e_1.astype(BF16), row(norm_final), tm=1024, tf=512)
    return xf.reshape(bsz, seq, D_MODEL)
```

```python
import functools

import jax
import jax.numpy as jnp
import numpy as np
from jax import lax
from jax.experimental import pallas as pl
from jax.experimental.pallas import tpu as pltpu

D_MODEL = 1024
HEAD_DIM = 64
MOBA_HEADS = 8
MOBA_BLOCK = 256
MOBA_TOPK = 3
DIL_DILATIONS = (1, 4, 16)
DIL_STEPS = 128
DIL_HEADS_PER_GROUP = 4
ROPE_THETA = 10000.0
RMS_EPS = 1e-6
N_EXPERTS = 8

WIDTH_A = MOBA_HEADS * HEAD_DIM
WIDTH_BG = DIL_HEADS_PER_GROUP * HEAD_DIM
WIDTH_B = len(DIL_DILATIONS) * WIDTH_BG

LANES = 128
NEG = -1e30
VMEM_LIMIT = 56 * 1024 * 1024

COL_GA = 0
COL_GB = COL_GA + D_MODEL
COL_QA = COL_GB + D_MODEL
COL_KA = COL_QA + WIDTH_A
COL_VA = COL_KA + WIDTH_A
COL_QB = COL_VA + WIDTH_A
COL_KB = COL_QB + WIDTH_B
COL_VB = COL_KB + WIDTH_B
PROJ_COLS = COL_VB + WIDTH_B

F32 = jnp.float32
BF16 = jnp.bfloat16


def _params(*sem):
    return pltpu.CompilerParams(dimension_semantics=sem, vmem_limit_bytes=VMEM_LIMIT)


def _dot(a, b):
    return jnp.dot(a, b, preferred_element_type=F32)


def _dot_nt(a, b):
    return lax.dot_general(a, b, (((1,), (1,)), ((), ())), preferred_element_type=F32)


def _rms_norm(x, g):
    return x * lax.rsqrt(jnp.mean(x * x, axis=-1, keepdims=True) + RMS_EPS) * g


def _sigmoid(z):
    return 1.0 / (1.0 + jnp.exp(-z))


def _rope_apply(acc, cos, sin_signed):
    width = acc.shape[-1]
    lane = lax.broadcasted_iota(jnp.int32, acc.shape, 1)
    first_half = (lane % HEAD_DIM) < (HEAD_DIM // 2)
    rot = jnp.where(first_half,
                    pltpu.roll(acc, width - HEAD_DIM // 2, 1),
                    pltpu.roll(acc, HEAD_DIM // 2, 1))
    return acc * cos + rot * sin_signed


def _in_proj_kernel(x_ref, g_ref, w_ref, cos_ref, sin_ref, o_ref):
    h = _rms_norm(x_ref[...], g_ref[...]).astype(BF16)
    plain = ((COL_GA, D_MODEL), (COL_GB, D_MODEL), (COL_VA, WIDTH_A), (COL_VB, WIDTH_B))
    rotary = ((COL_QA, WIDTH_A), (COL_KA, WIDTH_A), (COL_QB, WIDTH_B), (COL_KB, WIDTH_B))
    for c0, w in plain:
        o_ref[:, c0:c0 + w] = _dot(h, w_ref[:, c0:c0 + w]).astype(BF16)
    for c0, w in rotary:
        acc = _dot(h, w_ref[:, c0:c0 + w])
        o_ref[:, c0:c0 + w] = _rope_apply(acc, cos_ref[:, :w], sin_ref[:, :w]).astype(BF16)


def _in_proj(x, g, w, cos, sin, tm):
    n = x.shape[0]
    seq = cos.shape[0]
    n_seq_tiles = seq // tm
    return pl.pallas_call(
        _in_proj_kernel,
        grid=(n // tm,),
        in_specs=[
            pl.BlockSpec((tm, D_MODEL), lambda i: (i, 0)),
            pl.BlockSpec((1, D_MODEL), lambda i: (0, 0)),
            pl.BlockSpec((D_MODEL, PROJ_COLS), lambda i: (0, 0), pipeline_mode=pl.Buffered(1)),
            pl.BlockSpec((tm, WIDTH_B), lambda i: (i % n_seq_tiles, 0)),
            pl.BlockSpec((tm, WIDTH_B), lambda i: (i % n_seq_tiles, 0)),
        ],
        out_specs=pl.BlockSpec((tm, PROJ_COLS), lambda i: (i, 0)),
        out_shape=jax.ShapeDtypeStruct((n, PROJ_COLS), BF16),
        compiler_params=_params("parallel"),
        name="in_proj",
    )(x, g, w, cos, sin)


def _moba_kernel(q_ref, k_ref, v_ref, o_ref, qa_ref, qb_ref, ka_ref, kb_ref, va_ref, vb_ref, *, seq):
    n_blk = seq // MOBA_BLOCK
    q2 = q_ref[...]
    k2 = k_ref[...]
    lane = lax.broadcasted_iota(jnp.int32, (seq, LANES), 1)
    row = lax.broadcasted_iota(jnp.int32, (seq, LANES), 0)
    is_a = lane < HEAD_DIM

    kmean = jnp.mean(k2.astype(F32).reshape(n_blk, MOBA_BLOCK, LANES), axis=1)
    lane8 = lax.broadcasted_iota(jnp.int32, (n_blk, LANES), 1)
    km = jnp.concatenate([jnp.where(lane8 < HEAD_DIM, kmean, 0.0),
                          jnp.where(lane8 >= HEAD_DIM, kmean, 0.0)], axis=0)
    km_hi = km.astype(BF16)
    km_lo = (km - km_hi.astype(F32)).astype(BF16)
    g_all = _dot_nt(jnp.concatenate([km_hi, km_lo], axis=0), q2)
    gates = (g_all[0:n_blk] + g_all[2 * n_blk:3 * n_blk],
             g_all[n_blk:2 * n_blk] + g_all[3 * n_blk:4 * n_blk])

    blk = lax.broadcasted_iota(jnp.int32, (n_blk, seq), 0)
    q_blk = lax.broadcasted_iota(jnp.int32, (n_blk, seq), 1) // MOBA_BLOCK
    past = blk < q_blk
    bias_t = []
    for g in gates:
        g = jnp.where(past, g, -jnp.inf)
        rank = jnp.zeros((n_blk, seq), jnp.int32)
        for jp in range(n_blk):
            gj = g[jp:jp + 1, :]
            beats = (gj > g) | ((gj == g) & (jp < blk))
            rank = rank + beats.astype(jnp.int32)
        keep = ((rank < MOBA_TOPK) & past) | (blk == q_blk)
        bias_t.append(jnp.where(keep, 0.0, NEG))
    pad = jnp.zeros((HEAD_DIM - n_blk, seq), F32)
    bias = jnp.concatenate([bias_t[1], pad, bias_t[0], pad], axis=0).T.astype(BF16)
    key_blk = row // MOBA_BLOCK
    onehot = ((lane == key_blk) | (lane == key_blk + HEAD_DIM)).astype(BF16)

    v2 = v_ref[...]
    ones = jnp.ones_like(v2)
    qa_ref[...] = jnp.where(is_a, q2, bias)
    qb_ref[...] = jnp.where(is_a, bias, q2)
    ka_ref[...] = jnp.where(is_a, k2, onehot)
    kb_ref[...] = jnp.where(is_a, onehot, k2)
    va_ref[...] = jnp.where(is_a, v2, ones)
    vb_ref[...] = jnp.where(is_a, ones, v2)

    tri_r = lax.broadcasted_iota(jnp.int32, (MOBA_BLOCK, MOBA_BLOCK), 0)
    tri_c = lax.broadcasted_iota(jnp.int32, (MOBA_BLOCK, MOBA_BLOCK), 1)
    causal = tri_c <= tri_r
    out_lane = lax.broadcasted_iota(jnp.int32, (MOBA_BLOCK, LANES), 1) < HEAD_DIM

    def rows(j):
        return slice(j * MOBA_BLOCK, (j + 1) * MOBA_BLOCK)

    def half_max(s):
        return jnp.maximum(s[:, :LANES], s[:, LANES:])

    def head(qx_ref, kx_ref, vx_ref, i):
        qx = qx_ref[rows(i), :]
        s_own = jnp.where(causal, _dot_nt(qx, kx_ref[rows(i), :]), NEG)
        m_run = half_max(s_own)
        for j in range(i):
            m_run = jnp.maximum(m_run, half_max(_dot_nt(qx, kx_ref[rows(j), :])))
        m = jnp.max(m_run, axis=1, keepdims=True)
        acc = _dot(jnp.exp(s_own - m).astype(BF16), vx_ref[rows(i), :])
        for j in range(i):
            p = jnp.exp(_dot_nt(qx, kx_ref[rows(j), :]) - m)
            acc = acc + _dot(p.astype(BF16), vx_ref[rows(j), :])
        return acc * (1.0 / pltpu.roll(acc, HEAD_DIM, 1))

    for i in range(n_blk):
        out = jnp.where(out_lane, head(qa_ref, ka_ref, va_ref, i), head(qb_ref, kb_ref, vb_ref, i))
        o_ref[rows(i), :] = out.astype(o_ref.dtype)


def _moba(proj, bsz, seq):
    n = bsz * seq
    qc, kc, vc = COL_QA // LANES, COL_KA // LANES, COL_VA // LANES
    blk = (seq, LANES)
    return pl.pallas_call(
        functools.partial(_moba_kernel, seq=seq),
        grid=(bsz, WIDTH_A // LANES),
        in_specs=[
            pl.BlockSpec(blk, lambda b, h: (b, qc + h)),
            pl.BlockSpec(blk, lambda b, h: (b, kc + h)),
            pl.BlockSpec(blk, lambda b, h: (b, vc + h)),
        ],
        out_specs=pl.BlockSpec(blk, lambda b, h: (b, h)),
        out_shape=jax.ShapeDtypeStruct((n, WIDTH_A), BF16),
        scratch_shapes=[pltpu.VMEM(blk, BF16) for _ in range(6)],
        compiler_params=_params("parallel", "parallel"),
        name="moba",
    )(proj, proj, proj)


def _dilated_kernel(*refs, seq):
    n_groups = len(DIL_DILATIONS)
    in_refs = refs[:3 * n_groups]
    o_ref = refs[3 * n_groups]
    slab_ref, qres_ref, kres_ref, vres_ref, on_ref, ln_ref = refs[3 * n_groups + 1:]
    steps = DIL_STEPS
    n_pairs = WIDTH_BG // LANES
    n_row_blocks = seq // steps

    zeros_pad = jnp.zeros((steps, WIDTH_BG), BF16)
    kres_ref[0:steps, :] = zeros_pad
    vres_ref[0:steps, :] = zeros_pad

    qi = lax.broadcasted_iota(jnp.int32, (steps, 2 * steps), 0)
    kj = lax.broadcasted_iota(jnp.int32, (steps, 2 * steps), 1)
    band = (kj >= qi) & (kj <= qi + steps)
    cur_half = kj >= steps
    is_a = lax.broadcasted_iota(jnp.int32, (steps, LANES), 1) < HEAD_DIM

    for g, d in enumerate(DIL_DILATIONS):
        sub = seq // d
        n_blk = sub // steps
        q_ref, k_ref, v_ref = in_refs[3 * g:3 * g + 3]
        for src, dst, off in ((q_ref, qres_ref, 0), (k_ref, kres_ref, steps), (v_ref, vres_ref, steps)):
            if d == 1:
                dst[off:off + seq, :] = src[...]
                continue
            for hp in range(n_pairs):
                cols = slice(hp * LANES, (hp + 1) * LANES)
                slab_ref[...] = src[:, cols].astype(F32)
                for r in range(d):
                    dst[off + r * sub:off + (r + 1) * sub, cols] = (
                        slab_ref[pl.ds(r, sub, stride=d), :].astype(BF16))

        def row_block(c, _, d=d, n_blk=n_blk, g=g):
            rows_q = pl.ds(pl.multiple_of(c * steps, steps), steps)
            rows_kv = pl.ds(pl.multiple_of(c * steps, steps), 2 * steps)
            prev_valid = (c % n_blk) != 0
            mask = band & (cur_half | prev_valid)
            start = (c % n_blk) * (steps * d) + c // n_blk
            for hp in range(n_pairs):
                cols = slice(hp * LANES, (hp + 1) * LANES)
                q2 = qres_ref[rows_q, cols]
                k2 = kres_ref[rows_kv, cols]
                v2 = vres_ref[rows_kv, cols]
                outs, lses = [], []
                for head_a in (True, False):
                    qm = jnp.where(is_a if head_a else ~is_a, q2, jnp.zeros_like(q2))
                    s = jnp.where(mask, _dot_nt(qm, k2), NEG)
                    m = jnp.max(s, axis=1, keepdims=True)
                    p = jnp.exp(s - m)
                    l = jnp.sum(p, axis=1, keepdims=True)
                    outs.append(_dot(p.astype(BF16), v2) * (1.0 / l))
                    lses.append(jnp.broadcast_to(m + jnp.log(l), (steps, LANES)))
                o2 = jnp.where(is_a, outs[0], outs[1])
                l2 = jnp.where(is_a, lses[0], lses[1])
                slot = g * n_pairs + hp
                if d == 1:
                    on_ref[slot, rows_q, :] = o2
                    ln_ref[slot, rows_q, :] = l2
                else:
                    on_ref[slot, pl.ds(start, steps, stride=d), :] = o2
                    ln_ref[slot, pl.ds(start, steps, stride=d), :] = l2
            return 0

        lax.fori_loop(0, n_row_blocks, row_block, 0)

    for hp in range(n_pairs):
        ls = [ln_ref[g * n_pairs + hp] for g in range(n_groups)]
        mx = functools.reduce(jnp.maximum, ls)
        es = [jnp.exp(l - mx) for l in ls]
        inv = 1.0 / functools.reduce(lambda a, b: a + b, es)
        out = functools.reduce(lambda a, b: a + b,
                               [(es[g] * inv) * on_ref[g * n_pairs + hp] for g in range(n_groups)])
        o_ref[:, hp * LANES:(hp + 1) * LANES] = out.astype(o_ref.dtype)


def _dilated(proj, bsz, seq):
    n = bsz * seq
    n_groups = len(DIL_DILATIONS)
    blk = (seq, WIDTH_BG)
    in_specs = []
    for g in range(n_groups):
        for col in (COL_QB, COL_KB, COL_VB):
            cb = col // WIDTH_BG + g
            in_specs.append(pl.BlockSpec(blk, lambda b, cb=cb: (b, cb)))
    n_slots = n_groups * (WIDTH_BG // LANES)
    return pl.pallas_call(
        functools.partial(_dilated_kernel, seq=seq),
        grid=(bsz,),
        in_specs=in_specs,
        out_specs=pl.BlockSpec(blk, lambda b: (b, 0)),
        out_shape=jax.ShapeDtypeStruct((n, WIDTH_BG), BF16),
        scratch_shapes=[
            pltpu.VMEM((seq, LANES), F32),
            pltpu.VMEM((seq, WIDTH_BG), BF16),
            pltpu.VMEM((seq + DIL_STEPS, WIDTH_BG), BF16),
            pltpu.VMEM((seq + DIL_STEPS, WIDTH_BG), BF16),
            pltpu.VMEM((n_slots, seq, LANES), F32),
            pltpu.VMEM((n_slots, seq, LANES), F32),
        ],
        compiler_params=_params("parallel"),
        name="dilated",
    )(*([proj] * (3 * n_groups)))


def _merge_kernel(x_ref, oa_ref, ob_ref, ga_ref, gb_ref, wa_ref, wb_ref, wo_ref, o_ref):
    ya = _dot(oa_ref[...], wa_ref[...])
    yb = _dot(ob_ref[...], wb_ref[...])
    merged = _sigmoid(ga_ref[...].astype(F32)) * ya + _sigmoid(gb_ref[...].astype(F32)) * yb
    o_ref[...] = x_ref[...] + _dot(merged.astype(BF16), wo_ref[...])


def _merge(x, oa, ob, proj, wa, wb, wo, tm):
    n = x.shape[0]
    const = lambda i: (0, 0)
    return pl.pallas_call(
        _merge_kernel,
        grid=(n // tm,),
        in_specs=[
            pl.BlockSpec((tm, D_MODEL), lambda i: (i, 0)),
            pl.BlockSpec((tm, WIDTH_A), lambda i: (i, 0)),
            pl.BlockSpec((tm, WIDTH_BG), lambda i: (i, 0)),
            pl.BlockSpec((tm, D_MODEL), lambda i: (i, COL_GA // D_MODEL)),
            pl.BlockSpec((tm, D_MODEL), lambda i: (i, COL_GB // D_MODEL)),
            pl.BlockSpec((WIDTH_A, D_MODEL), const),
            pl.BlockSpec((WIDTH_BG, D_MODEL), const),
            pl.BlockSpec((D_MODEL, D_MODEL), const),
        ],
        out_specs=pl.BlockSpec((tm, D_MODEL), lambda i: (i, 0)),
        out_shape=jax.ShapeDtypeStruct((n, D_MODEL), F32),
        compiler_params=_params("parallel"),
        name="merge",
    )(x, oa, ob, proj, proj, wa, wb, wo)


def _swiglu_kernel(x_ref, g_ref, wg_ref, wu_ref, wd_ref, o_ref, h_ref, acc_ref):
    f = pl.program_id(1)

    @pl.when(f == 0)
    def _():
        h_ref[...] = _rms_norm(x_ref[...], g_ref[...]).astype(BF16)
        acc_ref[...] = jnp.zeros_like(acc_ref)

    h = h_ref[...]
    a = _dot(h, wg_ref[...])
    b = _dot(h, wu_ref[...])
    t = (a * _sigmoid(a) * b).astype(BF16)
    acc_ref[...] += _dot(t, wd_ref[...])

    @pl.when(f == pl.num_programs(1) - 1)
    def _():
        o_ref[...] = x_ref[...] + acc_ref[...]


def _swiglu(x, g, wg, wu, wd, tm, tf):
    n = x.shape[0]
    d_ff = wg.shape[1]
    return pl.pallas_call(
        _swiglu_kernel,
        grid=(n // tm, d_ff // tf),
        in_specs=[
            pl.BlockSpec((tm, D_MODEL), lambda i, f: (i, 0)),
            pl.BlockSpec((1, D_MODEL), lambda i, f: (0, 0)),
            pl.BlockSpec((D_MODEL, tf), lambda i, f: (0, f)),
            pl.BlockSpec((D_MODEL, tf), lambda i, f: (0, f)),
            pl.BlockSpec((tf, D_MODEL), lambda i, f: (f, 0)),
        ],
        out_specs=pl.BlockSpec((tm, D_MODEL), lambda i, f: (i, 0)),
        out_shape=jax.ShapeDtypeStruct((n, D_MODEL), F32),
        scratch_shapes=[pltpu.VMEM((tm, D_MODEL), BF16), pltpu.VMEM((tm, D_MODEL), F32)],
        compiler_params=_params("parallel", "arbitrary"),
        name="swiglu",
    )(x, g, wg, wu, wd)


def _router_combine(h, wr):
    logits = jnp.dot(h, wr, preferred_element_type=F32, precision=lax.Precision.HIGHEST)
    lane = lax.broadcasted_iota(jnp.int32, logits.shape, 1)
    logits = jnp.where(lane < N_EXPERTS, logits, -jnp.inf)
    v1 = jnp.max(logits, axis=1, keepdims=True)
    i1 = jnp.min(jnp.where(logits == v1, lane, LANES), axis=1, keepdims=True)
    rest = jnp.where(lane == i1, -jnp.inf, logits)
    v2 = jnp.max(rest, axis=1, keepdims=True)
    i2 = jnp.min(jnp.where(rest == v2, lane, LANES), axis=1, keepdims=True)
    e2 = jnp.exp(v2 - v1)
    w1 = 1.0 / (1.0 + e2)
    w2 = e2 / (1.0 + e2)
    return jnp.where(lane == i1, w1, 0.0) + jnp.where(lane == i2, w2, 0.0)


def _moe_kernel(x_ref, g_ref, wr_ref, wg_ref, wu_ref, wd_ref, gf_ref, o_ref, h_ref, acc_ref, comb_ref):
    e = pl.program_id(1)
    f = pl.program_id(2)
    first = (e == 0) & (f == 0)
    last = (e == pl.num_programs(1) - 1) & (f == pl.num_programs(2) - 1)

    @pl.when(first)
    def _():
        h = _rms_norm(x_ref[...], g_ref[...])
        h_ref[...] = h.astype(BF16)
        comb_ref[...] = _router_combine(h, wr_ref[...])
        acc_ref[...] = jnp.zeros_like(acc_ref)

    h = h_ref[...]
    a = _dot(h, wg_ref[0])
    b = _dot(h, wu_ref[0])
    comb = comb_ref[...]
    lane = lax.broadcasted_iota(jnp.int32, comb.shape, 1)
    c_e = jnp.sum(jnp.where(lane == e, comb, 0.0), axis=1, keepdims=True)
    t = (a * _sigmoid(a) * b * c_e).astype(BF16)
    acc_ref[...] += _dot(t, wd_ref[0])

    @pl.when(last)
    def _():
        o_ref[...] = _rms_norm(x_ref[...] + acc_ref[...], gf_ref[...])


def _moe(x, g, wr, wg, wu, wd, g_final, tm, tf):
    n = x.shape[0]
    n_exp, _, d_ff = wg.shape
    return pl.pallas_call(
        _moe_kernel,
        grid=(n // tm, n_exp, d_ff // tf),
        in_specs=[
            pl.BlockSpec((tm, D_MODEL), lambda i, e, f: (i, 0)),
            pl.BlockSpec((1, D_MODEL), lambda i, e, f: (0, 0)),
            pl.BlockSpec((D_MODEL, LANES), lambda i, e, f: (0, 0)),
            pl.BlockSpec((1, D_MODEL, tf), lambda i, e, f: (e, 0, f)),
            pl.BlockSpec((1, D_MODEL, tf), lambda i, e, f: (e, 0, f)),
            pl.BlockSpec((1, tf, D_MODEL), lambda i, e, f: (e, f, 0)),
            pl.BlockSpec((1, D_MODEL), lambda i, e, f: (0, 0)),
        ],
        out_specs=pl.BlockSpec((tm, D_MODEL), lambda i, e, f: (i, 0)),
        out_shape=jax.ShapeDtypeStruct((n, D_MODEL), F32),
        scratch_shapes=[pltpu.VMEM((tm, D_MODEL), BF16), pltpu.VMEM((tm, D_MODEL), F32),
                        pltpu.VMEM((tm, LANES), F32)],
        compiler_params=_params("parallel", "arbitrary", "arbitrary"),
        name="moe",
    )(x, g, wr, wg, wu, wd, g_final)


def _prep_w_in(w_in):
    sizes = [WIDTH_A] * 3 + [WIDTH_B] * 3 + [D_MODEL] * 2
    qa, ka, va, qb, kb, vb, ga, gb = jnp.split(w_in, np.cumsum(sizes)[:-1].tolist(), axis=1)
    scale = HEAD_DIM ** -0.5
    return jnp.concatenate([ga, gb, qa * scale, ka, va, qb * scale, kb, vb], axis=1).astype(BF16)


def _rope_tables(seq):
    half = HEAD_DIM // 2
    inv = ROPE_THETA ** (-jnp.arange(half, dtype=F32) / half)
    ang = jnp.arange(seq, dtype=F32)[:, None] * inv[None, :]
    cos, sin = jnp.cos(ang), jnp.sin(ang)
    n_heads = WIDTH_B // HEAD_DIM
    return (jnp.tile(jnp.concatenate([cos, cos], axis=1), (1, n_heads)),
            jnp.tile(jnp.concatenate([-sin, sin], axis=1), (1, n_heads)))


def kernel(x, norm_mix_0, w_in_0, w_proj_a_0, w_proj_b_0, w_out_0, norm_ffn_0, w_gate_0, w_up_0, w_down_0,
           norm_mix_1, w_in_1, w_proj_a_1, w_proj_b_1, w_out_1, norm_ffn_1, w_router_1, w_gate_e_1, w_up_e_1,
           w_down_e_1, norm_final):
    bsz, seq, _ = x.shape
    assert seq % MOBA_BLOCK == 0 and seq % (DIL_STEPS * max(DIL_DILATIONS)) == 0
    cos, sin = _rope_tables(seq)
    row = lambda v: v.reshape(1, D_MODEL).astype(F32)
    xf = x.reshape(bsz * seq, D_MODEL)

    def mixer(xf, nm, w_in, wa, wb, wo):
        proj = _in_proj(xf, row(nm), _prep_w_in(w_in), cos, sin, tm=512)
        oa = _moba(proj, bsz, seq)
        ob = _dilated(proj, bsz, seq)
        return _merge(xf, oa, ob, proj, wa.astype(BF16), wb.astype(BF16), wo.astype(BF16), tm=512)

    xf = mixer(xf, norm_mix_0, w_in_0, w_proj_a_0, w_proj_b_0, w_out_0)
    xf = _swiglu(xf, row(norm_ffn_0), w_gate_0.astype(BF16), w_up_0.astype(BF16), w_down_0.astype(BF16),
                 tm=1024, tf=256)
    xf = mixer(xf, norm_mix_1, w_in_1, w_proj_a_1, w_proj_b_1, w_out_1)
    wr = jnp.pad(w_router_1.astype(F32), ((0, 0), (0, LANES - N_EXPERTS)))
    xf = _moe(xf, row(norm_ffn_1), wr, w_gate_e_1.astype(BF16), w_up_e_1.astype(BF16),
              w_down_e_1.astype(BF16), row(norm_final), tm=1024, tf=512)
    return xf.reshape(bsz, seq, D_MODEL)
```

```python
import functools

import jax
import jax.numpy as jnp
import numpy as np
from jax import lax
from jax.experimental import pallas as pl
from jax.experimental.pallas import tpu as pltpu

D_MODEL = 1024
HEAD_DIM = 64
MOBA_HEADS = 8
MOBA_BLOCK = 256
MOBA_TOPK = 3
DIL_DILATIONS = (1, 4, 16)
DIL_STEPS = 128
DIL_HEADS_PER_GROUP = 4
ROPE_THETA = 10000.0
RMS_EPS = 1e-6
N_EXPERTS = 8

WIDTH_A = MOBA_HEADS * HEAD_DIM
WIDTH_BG = DIL_HEADS_PER_GROUP * HEAD_DIM
WIDTH_B = len(DIL_DILATIONS) * WIDTH_BG

LANES = 128
NEG = -1e30
VMEM_LIMIT = 56 * 1024 * 1024

COL_GA = 0
COL_GB = COL_GA + D_MODEL
COL_QA = COL_GB + D_MODEL
COL_KA = COL_QA + WIDTH_A
COL_VA = COL_KA + WIDTH_A
COL_QB = COL_VA + WIDTH_A
COL_KB = COL_QB + WIDTH_B
COL_VB = COL_KB + WIDTH_B
PROJ_COLS = COL_VB + WIDTH_B

F32 = jnp.float32
BF16 = jnp.bfloat16


def _params(*sem):
    return pltpu.CompilerParams(dimension_semantics=sem, vmem_limit_bytes=VMEM_LIMIT)


def _dot(a, b):
    return jnp.dot(a, b, preferred_element_type=F32)


def _dot_nt(a, b):
    return lax.dot_general(a, b, (((1,), (1,)), ((), ())), preferred_element_type=F32)


def _rms_norm(x, g):
    return x * lax.rsqrt(jnp.mean(x * x, axis=-1, keepdims=True) + RMS_EPS) * g


def _sigmoid(z):
    return 1.0 / (1.0 + jnp.exp(-z))


def _rope_apply(acc, cos, sin_signed):
    width = acc.shape[-1]
    lane = lax.broadcasted_iota(jnp.int32, acc.shape, 1)
    first_half = (lane % HEAD_DIM) < (HEAD_DIM // 2)
    rot = jnp.where(first_half,
                    pltpu.roll(acc, width - HEAD_DIM // 2, 1),
                    pltpu.roll(acc, HEAD_DIM // 2, 1))
    return acc * cos + rot * sin_signed


def _in_proj_kernel(x_ref, g_ref, w_ref, cos_ref, sin_ref, o_ref):
    h = _rms_norm(x_ref[...], g_ref[...]).astype(BF16)
    plain = ((COL_GA, D_MODEL), (COL_GB, D_MODEL), (COL_VA, WIDTH_A), (COL_VB, WIDTH_B))
    rotary = ((COL_QA, WIDTH_A), (COL_KA, WIDTH_A), (COL_QB, WIDTH_B), (COL_KB, WIDTH_B))
    for c0, w in plain:
        o_ref[:, c0:c0 + w] = _dot(h, w_ref[:, c0:c0 + w]).astype(BF16)
    for c0, w in rotary:
        acc = _dot(h, w_ref[:, c0:c0 + w])
        o_ref[:, c0:c0 + w] = _rope_apply(acc, cos_ref[:, :w], sin_ref[:, :w]).astype(BF16)


def _in_proj(x, g, w, cos, sin, tm):
    n = x.shape[0]
    seq = cos.shape[0]
    n_seq_tiles = seq // tm
    return pl.pallas_call(
        _in_proj_kernel,
        grid=(n // tm,),
        in_specs=[
            pl.BlockSpec((tm, D_MODEL), lambda i: (i, 0)),
            pl.BlockSpec((1, D_MODEL), lambda i: (0, 0)),
            pl.BlockSpec((D_MODEL, PROJ_COLS), lambda i: (0, 0), pipeline_mode=pl.Buffered(1)),
            pl.BlockSpec((tm, WIDTH_B), lambda i: (i % n_seq_tiles, 0)),
            pl.BlockSpec((tm, WIDTH_B), lambda i: (i % n_seq_tiles, 0)),
        ],
        out_specs=pl.BlockSpec((tm, PROJ_COLS), lambda i: (i, 0)),
        out_shape=jax.ShapeDtypeStruct((n, PROJ_COLS), BF16),
        compiler_params=_params("parallel"),
        name="in_proj",
    )(x, g, w, cos, sin)


def _moba_kernel(q_ref, k_ref, v_ref, o_ref, qa_ref, qb_ref, ka_ref, kb_ref, va_ref, vb_ref, *, seq):
    n_blk = seq // MOBA_BLOCK
    q2 = q_ref[...]
    k2 = k_ref[...]
    lane = lax.broadcasted_iota(jnp.int32, (seq, LANES), 1)
    row = lax.broadcasted_iota(jnp.int32, (seq, LANES), 0)
    is_a = lane < HEAD_DIM

    kmean = jnp.mean(k2.astype(F32).reshape(n_blk, MOBA_BLOCK, LANES), axis=1)
    lane8 = lax.broadcasted_iota(jnp.int32, (n_blk, LANES), 1)
    km = jnp.concatenate([jnp.where(lane8 < HEAD_DIM, kmean, 0.0),
                          jnp.where(lane8 >= HEAD_DIM, kmean, 0.0)], axis=0)
    km_hi = km.astype(BF16)
    km_lo = (km - km_hi.astype(F32)).astype(BF16)
    g_all = _dot_nt(jnp.concatenate([km_hi, km_lo], axis=0), q2)
    gates = (g_all[0:n_blk] + g_all[2 * n_blk:3 * n_blk],
             g_all[n_blk:2 * n_blk] + g_all[3 * n_blk:4 * n_blk])

    blk = lax.broadcasted_iota(jnp.int32, (n_blk, seq), 0)
    q_blk = lax.broadcasted_iota(jnp.int32, (n_blk, seq), 1) // MOBA_BLOCK
    past = blk < q_blk
    bias_t = []
    for g in gates:
        g = jnp.where(past, g, -jnp.inf)
        rank = jnp.zeros((n_blk, seq), jnp.int32)
        for jp in range(n_blk):
            gj = g[jp:jp + 1, :]
            beats = (gj > g) | ((gj == g) & (jp < blk))
            rank = rank + beats.astype(jnp.int32)
        keep = ((rank < MOBA_TOPK) & past) | (blk == q_blk)
        bias_t.append(jnp.where(keep, 0.0, NEG))
    pad = jnp.zeros((HEAD_DIM - n_blk, seq), F32)
    bias = jnp.concatenate([bias_t[1], pad, bias_t[0], pad], axis=0).T.astype(BF16)
    key_blk = row // MOBA_BLOCK
    onehot = ((lane == key_blk) | (lane == key_blk + HEAD_DIM)).astype(BF16)

    v2 = v_ref[...]
    ones = jnp.ones_like(v2)
    qa_ref[...] = jnp.where(is_a, q2, bias)
    qb_ref[...] = jnp.where(is_a, bias, q2)
    ka_ref[...] = jnp.where(is_a, k2, onehot)
    kb_ref[...] = jnp.where(is_a, onehot, k2)
    va_ref[...] = jnp.where(is_a, v2, ones)
    vb_ref[...] = jnp.where(is_a, ones, v2)

    tri_r = lax.broadcasted_iota(jnp.int32, (MOBA_BLOCK, MOBA_BLOCK), 0)
    tri_c = lax.broadcasted_iota(jnp.int32, (MOBA_BLOCK, MOBA_BLOCK), 1)
    causal = tri_c <= tri_r
    out_lane = lax.broadcasted_iota(jnp.int32, (MOBA_BLOCK, LANES), 1) < HEAD_DIM

    def rows(j):
        return slice(j * MOBA_BLOCK, (j + 1) * MOBA_BLOCK)

    def half_max(s):
        return jnp.maximum(s[:, :LANES], s[:, LANES:])

    def head(qx_ref, kx_ref, vx_ref, i):
        qx = qx_ref[rows(i), :]
        s_own = jnp.where(causal, _dot_nt(qx, kx_ref[rows(i), :]), NEG)
        m_run = half_max(s_own)
        for j in range(i):
            m_run = jnp.maximum(m_run, half_max(_dot_nt(qx, kx_ref[rows(j), :])))
        m = jnp.max(m_run, axis=1, keepdims=True)
        acc = _dot(jnp.exp(s_own - m).astype(BF16), vx_ref[rows(i), :])
        for j in range(i):
            p = jnp.exp(_dot_nt(qx, kx_ref[rows(j), :]) - m)
            acc = acc + _dot(p.astype(BF16), vx_ref[rows(j), :])
        return acc * (1.0 / pltpu.roll(acc, HEAD_DIM, 1))

    for i in range(n_blk):
        out = jnp.where(out_lane, head(qa_ref, ka_ref, va_ref, i), head(qb_ref, kb_ref, vb_ref, i))
        o_ref[rows(i), :] = out.astype(o_ref.dtype)


def _moba(proj, bsz, seq):
    n = bsz * seq
    qc, kc, vc = COL_QA // LANES, COL_KA // LANES, COL_VA // LANES
    blk = (seq, LANES)
    return pl.pallas_call(
        functools.partial(_moba_kernel, seq=seq),
        grid=(bsz, WIDTH_A // LANES),
        in_specs=[
            pl.BlockSpec(blk, lambda b, h: (b, qc + h)),
            pl.BlockSpec(blk, lambda b, h: (b, kc + h)),
            pl.BlockSpec(blk, lambda b, h: (b, vc + h)),
        ],
        out_specs=pl.BlockSpec(blk, lambda b, h: (b, h)),
        out_shape=jax.ShapeDtypeStruct((n, WIDTH_A), BF16),
        scratch_shapes=[pltpu.VMEM(blk, BF16) for _ in range(6)],
        compiler_params=_params("parallel", "parallel"),
        name="moba",
    )(proj, proj, proj)


def _dilated_kernel(*refs, seq):
    n_groups = len(DIL_DILATIONS)
    in_refs = refs[:3 * n_groups]
    o_ref = refs[3 * n_groups]
    slab_ref, qres_ref, kres_ref, vres_ref, on_ref, ln_ref = refs[3 * n_groups + 1:]
    steps = DIL_STEPS
    n_pairs = WIDTH_BG // LANES
    n_row_blocks = seq // steps

    zeros_pad = jnp.zeros((steps, WIDTH_BG), BF16)
    kres_ref[0:steps, :] = zeros_pad
    vres_ref[0:steps, :] = zeros_pad

    qi = lax.broadcasted_iota(jnp.int32, (steps, 2 * steps), 0)
    kj = lax.broadcasted_iota(jnp.int32, (steps, 2 * steps), 1)
    band = (kj >= qi) & (kj <= qi + steps)
    cur_half = kj >= steps
    is_a = lax.broadcasted_iota(jnp.int32, (steps, LANES), 1) < HEAD_DIM

    for g, d in enumerate(DIL_DILATIONS):
        sub = seq // d
        n_blk = sub // steps
        q_ref, k_ref, v_ref = in_refs[3 * g:3 * g + 3]
        for src, dst, off in ((q_ref, qres_ref, 0), (k_ref, kres_ref, steps), (v_ref, vres_ref, steps)):
            if d == 1:
                dst[off:off + seq, :] = src[...]
                continue
            for hp in range(n_pairs):
                cols = slice(hp * LANES, (hp + 1) * LANES)
                slab_ref[...] = src[:, cols].astype(F32)
                for r in range(d):
                    dst[off + r * sub:off + (r + 1) * sub, cols] = (
                        slab_ref[pl.ds(r, sub, stride=d), :].astype(BF16))

        def row_block(c, _, d=d, n_blk=n_blk, g=g):
            rows_q = pl.ds(pl.multiple_of(c * steps, steps), steps)
            rows_kv = pl.ds(pl.multiple_of(c * steps, steps), 2 * steps)
            prev_valid = (c % n_blk) != 0
            mask = band & (cur_half | prev_valid)
            start = (c % n_blk) * (steps * d) + c // n_blk
            for hp in range(n_pairs):
                cols = slice(hp * LANES, (hp + 1) * LANES)
                q2 = qres_ref[rows_q, cols]
                k2 = kres_ref[rows_kv, cols]
                v2 = vres_ref[rows_kv, cols]
                outs, lses = [], []
                for head_a in (True, False):
                    qm = jnp.where(is_a if head_a else ~is_a, q2, jnp.zeros_like(q2))
                    s = jnp.where(mask, _dot_nt(qm, k2), NEG)
                    m = jnp.max(s, axis=1, keepdims=True)
                    p = jnp.exp(s - m)
                    l = jnp.sum(p, axis=1, keepdims=True)
                    outs.append(_dot(p.astype(BF16), v2) * (1.0 / l))
                    lses.append(jnp.broadcast_to(m + jnp.log(l), (steps, LANES)))
                o2 = jnp.where(is_a, outs[0], outs[1])
                l2 = jnp.where(is_a, lses[0], lses[1])
                slot = g * n_pairs + hp
                if d == 1:
                    on_ref[slot, rows_q, :] = o2
                    ln_ref[slot, rows_q, :] = l2
                else:
                    on_ref[slot, pl.ds(start, steps, stride=d), :] = o2
                    ln_ref[slot, pl.ds(start, steps, stride=d), :] = l2
            return 0

        lax.fori_loop(0, n_row_blocks, row_block, 0)

    for hp in range(n_pairs):
        ls = [ln_ref[g * n_pairs + hp] for g in range(n_groups)]
        mx = functools.reduce(jnp.maximum, ls)
        es = [jnp.exp(l - mx) for l in ls]
        inv = 1.0 / functools.reduce(lambda a, b: a + b, es)
        out = functools.reduce(lambda a, b: a + b,
                               [(es[g] * inv) * on_ref[g * n_pairs + hp] for g in range(n_groups)])
        o_ref[:, hp * LANES:(hp + 1) * LANES] = out.astype(o_ref.dtype)


def _dilated(proj, bsz, seq):
    n = bsz * seq
    n_groups = len(DIL_DILATIONS)
    blk = (seq, WIDTH_BG)
    in_specs = []
    for g in range(n_groups):
        for col in (COL_QB, COL_KB, COL_VB):
            cb = col // WIDTH_BG + g
            in_specs.append(pl.BlockSpec(blk, lambda b, cb=cb: (b, cb)))
    n_slots = n_groups * (WIDTH_BG // LANES)
    return pl.pallas_call(
        functools.partial(_dilated_kernel, seq=seq),
        grid=(bsz,),
        in_specs=in_specs,
        out_specs=pl.BlockSpec(blk, lambda b: (b, 0)),
        out_shape=jax.ShapeDtypeStruct((n, WIDTH_BG), BF16),
        scratch_shapes=[
            pltpu.VMEM((seq, LANES), F32),
            pltpu.VMEM((seq, WIDTH_BG), BF16),
            pltpu.VMEM((seq + DIL_STEPS, WIDTH_BG), BF16),
            pltpu.VMEM((seq + DIL_STEPS, WIDTH_BG), BF16),
            pltpu.VMEM((n_slots, seq, LANES), F32),
            pltpu.VMEM((n_slots, seq, LANES), F32),
        ],
        compiler_params=_params("parallel"),
        name="dilated",
    )(*([proj] * (3 * n_groups)))


def _merge_kernel(x_ref, oa_ref, ob_ref, ga_ref, gb_ref, wa_ref, wb_ref, wo_ref, o_ref):
    ya = _dot(oa_ref[...], wa_ref[...])
    yb = _dot(ob_ref[...], wb_ref[...])
    merged = _sigmoid(ga_ref[...].astype(F32)) * ya + _sigmoid(gb_ref[...].astype(F32)) * yb
    o_ref[...] = x_ref[...] + _dot(merged.astype(BF16), wo_ref[...])


def _merge(x, oa, ob, proj, wa, wb, wo, tm):
    n = x.shape[0]
    const = lambda i: (0, 0)
    return pl.pallas_call(
        _merge_kernel,
        grid=(n // tm,),
        in_specs=[
            pl.BlockSpec((tm, D_MODEL), lambda i: (i, 0)),
            pl.BlockSpec((tm, WIDTH_A), lambda i: (i, 0)),
            pl.BlockSpec((tm, WIDTH_BG), lambda i: (i, 0)),
            pl.BlockSpec((tm, D_MODEL), lambda i: (i, COL_GA // D_MODEL)),
            pl.BlockSpec((tm, D_MODEL), lambda i: (i, COL_GB // D_MODEL)),
            pl.BlockSpec((WIDTH_A, D_MODEL), const),
            pl.BlockSpec((WIDTH_BG, D_MODEL), const),
            pl.BlockSpec((D_MODEL, D_MODEL), const),
        ],
        out_specs=pl.BlockSpec((tm, D_MODEL), lambda i: (i, 0)),
        out_shape=jax.ShapeDtypeStruct((n, D_MODEL), F32),
        compiler_params=_params("parallel"),
        name="merge",
    )(x, oa, ob, proj, proj, wa, wb, wo)


def _swiglu_kernel(x_ref, g_ref, wg_ref, wu_ref, wd_ref, o_ref, h_ref, acc_ref):
    f = pl.program_id(1)

    @pl.when(f == 0)
    def _():
        h_ref[...] = _rms_norm(x_ref[...], g_ref[...]).astype(BF16)
        acc_ref[...] = jnp.zeros_like(acc_ref)

    h = h_ref[...]
    a = _dot(h, wg_ref[...])
    b = _dot(h, wu_ref[...])
    t = (a * _sigmoid(a) * b).astype(BF16)
    acc_ref[...] += _dot(t, wd_ref[...])

    @pl.when(f == pl.num_programs(1) - 1)
    def _():
        o_ref[...] = x_ref[...] + acc_ref[...]


def _swiglu(x, g, wg, wu, wd, tm, tf):
    n = x.shape[0]
    d_ff = wg.shape[1]
    return pl.pallas_call(
        _swiglu_kernel,
        grid=(n // tm, d_ff // tf),
        in_specs=[
            pl.BlockSpec((tm, D_MODEL), lambda i, f: (i, 0)),
            pl.BlockSpec((1, D_MODEL), lambda i, f: (0, 0)),
            pl.BlockSpec((D_MODEL, tf), lambda i, f: (0, f)),
            pl.BlockSpec((D_MODEL, tf), lambda i, f: (0, f)),
            pl.BlockSpec((tf, D_MODEL), lambda i, f: (f, 0)),
        ],
        out_specs=pl.BlockSpec((tm, D_MODEL), lambda i, f: (i, 0)),
        out_shape=jax.ShapeDtypeStruct((n, D_MODEL), F32),
        scratch_shapes=[pltpu.VMEM((tm, D_MODEL), BF16), pltpu.VMEM((tm, D_MODEL), F32)],
        compiler_params=_params("parallel", "arbitrary"),
        name="swiglu",
    )(x, g, wg, wu, wd)


MOE_TM = 512
MOE_TG = 256
MOE_TC = 512
MOE_TF = 896
META_E1, META_E2, META_R1, META_R2, META_W1, META_W2 = range(6)


def _route_kernel(x_ref, g_ref, wr_ref, h_ref, meta_ref, metat_ref, cs_ref, tot_ref, carry_ref):
    @pl.when(pl.program_id(0) == 0)
    def _():
        carry_ref[...] = jnp.zeros_like(carry_ref)

    h = _rms_norm(x_ref[...], g_ref[...])
    h_ref[...] = h.astype(BF16)
    logits = jnp.dot(h, wr_ref[...], preferred_element_type=F32, precision=lax.Precision.HIGHEST)
    tc = logits.shape[0]
    lane = lax.broadcasted_iota(jnp.int32, logits.shape, 1)
    logits = jnp.where(lane < N_EXPERTS, logits, -jnp.inf)
    v1 = jnp.max(logits, axis=1, keepdims=True)
    i1 = jnp.min(jnp.where(logits == v1, lane, LANES), axis=1, keepdims=True)
    rest = jnp.where(lane == i1, -jnp.inf, logits)
    v2 = jnp.max(rest, axis=1, keepdims=True)
    i2 = jnp.min(jnp.where(rest == v2, lane, LANES), axis=1, keepdims=True)
    e2 = jnp.exp(v2 - v1)
    w1 = 1.0 / (1.0 + e2)
    w2 = e2 / (1.0 + e2)

    chosen = jnp.where((lane == i1) | (lane == i2), 1.0, 0.0)
    r_i = lax.broadcasted_iota(jnp.int32, (tc, tc), 0)
    c_i = lax.broadcasted_iota(jnp.int32, (tc, tc), 1)
    before = jnp.where(c_i < r_i, 1.0, 0.0).astype(BF16)
    carry = carry_ref[0:1, :]
    rank = _dot(before, chosen.astype(BF16)) + carry
    r1 = jnp.sum(jnp.where(lane == i1, rank, 0.0), axis=1, keepdims=True)
    r2 = jnp.sum(jnp.where(lane == i2, rank, 0.0), axis=1, keepdims=True)
    cols = (i1.astype(F32), i2.astype(F32), r1, r2, w1, w2)
    meta = jnp.zeros(logits.shape, F32)
    for idx, col in enumerate(cols):
        meta = jnp.where(lane == idx, col, meta)
    meta_ref[...] = meta
    metat_ref[...] = meta.T[0:8, :]
    cs_ref[0] = jnp.broadcast_to(carry, (8, LANES))
    total = carry + jnp.sum(chosen, axis=0, keepdims=True)
    carry_ref[...] = jnp.broadcast_to(total, (8, LANES))
    tot_ref[...] = jnp.broadcast_to(total, (8, LANES))


def _route(x, g, wr):
    n = x.shape[0]
    tc = MOE_TC
    n_c = n // tc
    return pl.pallas_call(
        _route_kernel,
        grid=(n_c,),
        in_specs=[
            pl.BlockSpec((tc, D_MODEL), lambda c: (c, 0)),
            pl.BlockSpec((1, D_MODEL), lambda c: (0, 0)),
            pl.BlockSpec((D_MODEL, LANES), lambda c: (0, 0)),
        ],
        out_specs=[
            pl.BlockSpec((tc, D_MODEL), lambda c: (c, 0)),
            pl.BlockSpec((tc, LANES), lambda c: (c, 0)),
            pl.BlockSpec((8, tc), lambda c: (0, c)),
            pl.BlockSpec((1, 8, LANES), lambda c: (c, 0, 0)),
            pl.BlockSpec((8, LANES), lambda c: (0, 0)),
        ],
        out_shape=[
            jax.ShapeDtypeStruct((n, D_MODEL), BF16),
            jax.ShapeDtypeStruct((n, LANES), F32),
            jax.ShapeDtypeStruct((8, n), F32),
            jax.ShapeDtypeStruct((n_c, 8, LANES), F32),
            jax.ShapeDtypeStruct((8, LANES), F32),
        ],
        scratch_shapes=[pltpu.VMEM((8, LANES), F32)],
        compiler_params=_params("arbitrary"),
        name="route",
    )(x, g, wr)


def _moe_plan(cs, tot, n):
    tm, tg, tc = MOE_TM, MOE_TG, MOE_TC
    n_c = n // tc
    n_tiles = 2 * n // tm + N_EXPERTS
    n_g = n_tiles * tm // tg
    k_max = n_g + N_EXPERTS * n_c
    i32 = jnp.int32
    counts = tot[0, :N_EXPERTS].astype(i32)
    cs = cs[:, 0, :N_EXPERTS].astype(i32)
    padded = (counts + tm - 1) // tm * tm
    ends = jnp.cumsum(padded)
    off = ends - padded
    tile_start = jnp.arange(n_tiles, dtype=i32) * tm
    tile_e = jnp.minimum(jnp.sum(tile_start[:, None] >= ends[None, :], axis=1), N_EXPERTS - 1).astype(i32)
    n_valid_tiles = (ends[-1] // tm).astype(i32).reshape(1)

    p0 = jnp.arange(n_g, dtype=i32) * tg
    g_e = tile_e[p0 // tm]
    lo = p0 - off[g_e]
    cs_g = cs[:, g_e].T
    c_lo = jnp.maximum(jnp.sum(cs_g <= lo[:, None], axis=1) - 1, 0)
    c_hi = jnp.maximum(jnp.sum(cs_g < (lo + tg)[:, None], axis=1) - 1, c_lo)
    n_it = c_hi - c_lo + 1
    it_end = jnp.cumsum(n_it)
    it_start = it_end - n_it
    n_items = it_end[-1].astype(i32).reshape(1)
    k = jnp.arange(k_max, dtype=i32)
    valid = k < n_items[0]
    item_g = jnp.minimum(jnp.sum(k[:, None] >= it_end[None, :], axis=1), n_g - 1).astype(i32)
    item_c = jnp.minimum(c_lo[item_g] + k - it_start[item_g], c_hi[item_g]).astype(i32)
    item_first = ((k == it_start[item_g]) & valid).astype(i32)

    order = jnp.argsort(jnp.where(valid, item_c * n_g + item_g, n_c * n_g + k))
    last_valid = n_items[0] - 1
    cc = item_c[order]
    cg = item_g[order]
    cc = jnp.where(valid, cc, cc[last_valid]).astype(i32)
    cg = jnp.where(valid, cg, cg[last_valid]).astype(i32)
    prev_c = jnp.concatenate([jnp.full((1,), -1, i32), cc[:-1]])
    next_c = jnp.concatenate([cc[1:], jnp.full((1,), -1, i32)])
    comb_first = (valid & (cc != prev_c)).astype(i32)
    comb_last = (valid & ((cc != next_c) | (k == last_valid))).astype(i32)
    return dict(tile_e=tile_e, n_valid_tiles=n_valid_tiles, g_e=g_e.astype(i32), off=off.astype(i32),
                n_items=n_items, item_g=item_g, item_c=item_c, item_first=item_first,
                comb_c=cc, comb_g=cg, comb_first=comb_first, comb_last=comb_last,
                n_tiles=n_tiles, n_g=n_g, k_max=k_max)


def _gather_kernel(ig_ref, ic_ref, first_ref, n_ref, ge_ref, off_ref, h_ref, mt_ref, o_ref):
    k = pl.program_id(0)
    tg, tc = o_ref.shape[0], h_ref.shape[0]

    @pl.when(k < n_ref[0])
    def _():
        g = ig_ref[k]
        e = ge_ref[g]
        base = (off_ref[e] - g * tg).astype(F32)
        ef = e.astype(F32)
        mt = mt_ref[...]
        tgt1 = jnp.where(mt[META_E1:META_E1 + 1] == ef, mt[META_R1:META_R1 + 1] + base, -1.0)
        tgt2 = jnp.where(mt[META_E2:META_E2 + 1] == ef, mt[META_R2:META_R2 + 1] + base, -1.0)
        row = lax.broadcasted_iota(jnp.int32, (tg, tc), 0).astype(F32)
        pick = jnp.where((row == tgt1) | (row == tgt2), 1.0, 0.0).astype(BF16)
        part = _dot(pick, h_ref[...]).astype(BF16)

        @pl.when(first_ref[k] == 1)
        def _():
            o_ref[...] = part

        @pl.when(first_ref[k] == 0)
        def _():
            o_ref[...] += part


def _gather(plan, h, metat):
    tg, tc = MOE_TG, MOE_TC
    grid_spec = pltpu.PrefetchScalarGridSpec(
        num_scalar_prefetch=6,
        grid=(plan["k_max"],),
        in_specs=[
            pl.BlockSpec((tc, D_MODEL), lambda k, ig, ic, *_: (ic[k], 0)),
            pl.BlockSpec((8, tc), lambda k, ig, ic, *_: (0, ic[k])),
        ],
        out_specs=pl.BlockSpec((tg, D_MODEL), lambda k, ig, ic, *_: (ig[k], 0)),
    )
    return pl.pallas_call(
        _gather_kernel,
        grid_spec=grid_spec,
        out_shape=jax.ShapeDtypeStruct((plan["n_g"] * tg, D_MODEL), BF16),
        compiler_params=_params("arbitrary"),
        name="moe_gather",
    )(plan["item_g"], plan["item_c"], plan["item_first"], plan["n_items"], plan["g_e"], plan["off"], h, metat)


def _ffn_kernel(te_ref, nv_ref, xs_ref, wg_ref, wu_ref, wd_ref, o_ref, acc_ref):
    i = pl.program_id(0)
    f = pl.program_id(1)
    valid = i < nv_ref[0]

    @pl.when(valid & (f == 0))
    def _():
        acc_ref[...] = jnp.zeros_like(acc_ref)

    @pl.when(valid)
    def _():
        xs = xs_ref[...]
        a = _dot(xs, wg_ref[0])
        b = _dot(xs, wu_ref[0])
        t = (a * _sigmoid(a) * b).astype(BF16)
        acc_ref[...] += _dot(t, wd_ref[0])

    last = f == pl.num_programs(1) - 1

    @pl.when(valid & last)
    def _():
        o_ref[...] = acc_ref[...].astype(o_ref.dtype)

    @pl.when(jnp.logical_not(valid) & last)
    def _():
        o_ref[...] = jnp.zeros_like(o_ref)


def _ffn(plan, xs, wg, wu, wd):
    tm, tf = MOE_TM, MOE_TF
    d_ff = wg.shape[2]
    n_f = d_ff // tf

    def f_eff(i, f, nv):
        return jnp.where(i < nv[0], f, n_f - 1)

    grid_spec = pltpu.PrefetchScalarGridSpec(
        num_scalar_prefetch=2,
        grid=(plan["n_tiles"], n_f),
        in_specs=[
            pl.BlockSpec((tm, D_MODEL), lambda i, f, te, nv: (i, 0)),
            pl.BlockSpec((1, D_MODEL, tf), lambda i, f, te, nv: (te[i], 0, f_eff(i, f, nv))),
            pl.BlockSpec((1, D_MODEL, tf), lambda i, f, te, nv: (te[i], 0, f_eff(i, f, nv))),
            pl.BlockSpec((1, tf, D_MODEL), lambda i, f, te, nv: (te[i], f_eff(i, f, nv), 0)),
        ],
        out_specs=pl.BlockSpec((tm, D_MODEL), lambda i, f, te, nv: (i, 0)),
        scratch_shapes=[pltpu.VMEM((tm, D_MODEL), F32)],
    )
    return pl.pallas_call(
        _ffn_kernel,
        grid_spec=grid_spec,
        out_shape=jax.ShapeDtypeStruct(xs.shape, BF16),
        compiler_params=_params("arbitrary", "arbitrary"),
        name="moe_ffn",
    )(plan["tile_e"], plan["n_valid_tiles"], xs, wg, wu, wd)


def _combine_kernel(cc_ref, cg_ref, first_ref, last_ref, n_ref, ge_ref, off_ref,
                    x_ref, meta_ref, y_ref, gf_ref, o_ref, acc_ref):
    k = pl.program_id(0)
    tg, tc = y_ref.shape[0], x_ref.shape[0]

    @pl.when(k < n_ref[0])
    def _():
        g = cg_ref[k]
        e = ge_ref[g]
        base = (off_ref[e] - g * tg).astype(F32)
        ef = e.astype(F32)
        meta = meta_ref[...]
        col = lambda c: meta[:, c:c + 1]
        tgt1 = jnp.where(col(META_E1) == ef, col(META_R1) + base, -1.0)
        tgt2 = jnp.where(col(META_E2) == ef, col(META_R2) + base, -1.0)
        pos = lax.broadcasted_iota(jnp.int32, (tc, tg), 1).astype(F32)
        spread = (jnp.where(pos == tgt1, col(META_W1), 0.0) + jnp.where(pos == tgt2, col(META_W2), 0.0))
        part = _dot(spread.astype(BF16), y_ref[...])

        @pl.when(first_ref[k] == 1)
        def _():
            acc_ref[...] = part

        @pl.when(first_ref[k] == 0)
        def _():
            acc_ref[...] += part

        @pl.when(last_ref[k] == 1)
        def _():
            o_ref[...] = _rms_norm(x_ref[...] + acc_ref[...], gf_ref[...])


def _combine(plan, x, meta, y, g_final):
    n = x.shape[0]
    tg, tc = MOE_TG, MOE_TC
    grid_spec = pltpu.PrefetchScalarGridSpec(
        num_scalar_prefetch=7,
        grid=(plan["k_max"],),
        in_specs=[
            pl.BlockSpec((tc, D_MODEL), lambda k, cc, cg, *_: (cc[k], 0)),
            pl.BlockSpec((tc, LANES), lambda k, cc, cg, *_: (cc[k], 0)),
            pl.BlockSpec((tg, D_MODEL), lambda k, cc, cg, *_: (cg[k], 0)),
            pl.BlockSpec((1, D_MODEL), lambda k, cc, cg, *_: (0, 0)),
        ],
        out_specs=pl.BlockSpec((tc, D_MODEL), lambda k, cc, cg, *_: (cc[k], 0)),
        scratch_shapes=[pltpu.VMEM((tc, D_MODEL), F32)],
    )
    return pl.pallas_call(
        _combine_kernel,
        grid_spec=grid_spec,
        out_shape=jax.ShapeDtypeStruct((n, D_MODEL), F32),
        compiler_params=_params("arbitrary"),
        name="moe_combine",
    )(plan["comb_c"], plan["comb_g"], plan["comb_first"], plan["comb_last"], plan["n_items"],
      plan["g_e"], plan["off"], x, meta, y, g_final)


def _moe(x, g, wr, wg, wu, wd, g_final):
    n = x.shape[0]
    h, meta, metat, cs, tot = _route(x, g, wr)
    plan = _moe_plan(cs, tot, n)
    xs = _gather(plan, h, metat)
    y = _ffn(plan, xs, wg, wu, wd)
    return _combine(plan, x, meta, y, g_final)


def _prep_w_in(w_in):
    sizes = [WIDTH_A] * 3 + [WIDTH_B] * 3 + [D_MODEL] * 2
    qa, ka, va, qb, kb, vb, ga, gb = jnp.split(w_in, np.cumsum(sizes)[:-1].tolist(), axis=1)
    scale = HEAD_DIM ** -0.5
    return jnp.concatenate([ga, gb, qa * scale, ka, va, qb * scale, kb, vb], axis=1).astype(BF16)


def _rope_tables(seq):
    half = HEAD_DIM // 2
    inv = ROPE_THETA ** (-jnp.arange(half, dtype=F32) / half)
    ang = jnp.arange(seq, dtype=F32)[:, None] * inv[None, :]
    cos, sin = jnp.cos(ang), jnp.sin(ang)
    n_heads = WIDTH_B // HEAD_DIM
    return (jnp.tile(jnp.concatenate([cos, cos], axis=1), (1, n_heads)),
            jnp.tile(jnp.concatenate([-sin, sin], axis=1), (1, n_heads)))


def kernel(x, norm_mix_0, w_in_0, w_proj_a_0, w_proj_b_0, w_out_0, norm_ffn_0, w_gate_0, w_up_0, w_down_0,
           norm_mix_1, w_in_1, w_proj_a_1, w_proj_b_1, w_out_1, norm_ffn_1, w_router_1, w_gate_e_1, w_up_e_1,
           w_down_e_1, norm_final):
    bsz, seq, _ = x.shape
    assert seq % MOBA_BLOCK == 0 and seq % (DIL_STEPS * max(DIL_DILATIONS)) == 0
    cos, sin = _rope_tables(seq)
    row = lambda v: v.reshape(1, D_MODEL).astype(F32)
    xf = x.reshape(bsz * seq, D_MODEL)

    def mixer(xf, nm, w_in, wa, wb, wo):
        proj = _in_proj(xf, row(nm), _prep_w_in(w_in), cos, sin, tm=512)
        oa = _moba(proj, bsz, seq)
        ob = _dilated(proj, bsz, seq)
        return _merge(xf, oa, ob, proj, wa.astype(BF16), wb.astype(BF16), wo.astype(BF16), tm=512)

    xf = mixer(xf, norm_mix_0, w_in_0, w_proj_a_0, w_proj_b_0, w_out_0)
    xf = _swiglu(xf, row(norm_ffn_0), w_gate_0.astype(BF16), w_up_0.astype(BF16), w_down_0.astype(BF16),
                 tm=1024, tf=256)
    xf = mixer(xf, norm_mix_1, w_in_1, w_proj_a_1, w_proj_b_1, w_out_1)
    wr = jnp.pad(w_router_1.astype(F32), ((0, 0), (0, LANES - N_EXPERTS)))
    xf = _moe(xf, row(norm_ffn_1), wr, w_gate_e_1.astype(BF16), w_up_e_1.astype(BF16),
              w_down_e_1.astype(BF16), row(norm_final))
    return xf.reshape(bsz, seq, D_MODEL)
```

```python
import functools

import jax
import jax.numpy as jnp
import numpy as np
from jax import lax
from jax.experimental import pallas as pl
from jax.experimental.pallas import tpu as pltpu

D_MODEL = 1024
HEAD_DIM = 64
MOBA_HEADS = 8
MOBA_BLOCK = 256
MOBA_TOPK = 3
DIL_DILATIONS = (1, 4, 16)
DIL_STEPS = 128
DIL_HEADS_PER_GROUP = 4
ROPE_THETA = 10000.0
RMS_EPS = 1e-6
N_EXPERTS = 8

WIDTH_A = MOBA_HEADS * HEAD_DIM
WIDTH_BG = DIL_HEADS_PER_GROUP * HEAD_DIM
WIDTH_B = len(DIL_DILATIONS) * WIDTH_BG

LANES = 128
NEG = -1e30
VMEM_LIMIT = 56 * 1024 * 1024

COL_GA = 0
COL_GB = COL_GA + D_MODEL
COL_QA = COL_GB + D_MODEL
COL_KA = COL_QA + WIDTH_A
COL_VA = COL_KA + WIDTH_A
COL_QB = COL_VA + WIDTH_A
COL_KB = COL_QB + WIDTH_B
COL_VB = COL_KB + WIDTH_B
PROJ_COLS = COL_VB + WIDTH_B

F32 = jnp.float32
BF16 = jnp.bfloat16


def _params(*sem):
    return pltpu.CompilerParams(dimension_semantics=sem, vmem_limit_bytes=VMEM_LIMIT)


def _dot(a, b):
    return jnp.dot(a, b, preferred_element_type=F32)


def _dot_nt(a, b):
    return lax.dot_general(a, b, (((1,), (1,)), ((), ())), preferred_element_type=F32)


def _rms_norm(x, g):
    return x * lax.rsqrt(jnp.mean(x * x, axis=-1, keepdims=True) + RMS_EPS) * g


def _sigmoid(z):
    return 1.0 / (1.0 + jnp.exp(-z))


def _rope_apply(acc, cos, sin_signed):
    width = acc.shape[-1]
    lane = lax.broadcasted_iota(jnp.int32, acc.shape, 1)
    first_half = (lane % HEAD_DIM) < (HEAD_DIM // 2)
    rot = jnp.where(first_half,
                    pltpu.roll(acc, width - HEAD_DIM // 2, 1),
                    pltpu.roll(acc, HEAD_DIM // 2, 1))
    return acc * cos + rot * sin_signed


def _in_proj_kernel(x_ref, g_ref, w_ref, cos_ref, sin_ref, o_ref):
    h = _rms_norm(x_ref[...], g_ref[...]).astype(BF16)
    plain = ((COL_GA, D_MODEL), (COL_GB, D_MODEL), (COL_VA, WIDTH_A), (COL_VB, WIDTH_B))
    rotary = ((COL_QA, WIDTH_A), (COL_KA, WIDTH_A), (COL_QB, WIDTH_B), (COL_KB, WIDTH_B))
    for c0, w in plain:
        o_ref[:, c0:c0 + w] = _dot(h, w_ref[:, c0:c0 + w]).astype(BF16)
    for c0, w in rotary:
        acc = _dot(h, w_ref[:, c0:c0 + w])
        o_ref[:, c0:c0 + w] = _rope_apply(acc, cos_ref[:, :w], sin_ref[:, :w]).astype(BF16)


def _in_proj(x, g, w, cos, sin, tm):
    n = x.shape[0]
    seq = cos.shape[0]
    n_seq_tiles = seq // tm
    return pl.pallas_call(
        _in_proj_kernel,
        grid=(n // tm,),
        in_specs=[
            pl.BlockSpec((tm, D_MODEL), lambda i: (i, 0)),
            pl.BlockSpec((1, D_MODEL), lambda i: (0, 0)),
            pl.BlockSpec((D_MODEL, PROJ_COLS), lambda i: (0, 0), pipeline_mode=pl.Buffered(1)),
            pl.BlockSpec((tm, WIDTH_B), lambda i: (i % n_seq_tiles, 0)),
            pl.BlockSpec((tm, WIDTH_B), lambda i: (i % n_seq_tiles, 0)),
        ],
        out_specs=pl.BlockSpec((tm, PROJ_COLS), lambda i: (i, 0)),
        out_shape=jax.ShapeDtypeStruct((n, PROJ_COLS), BF16),
        compiler_params=_params("parallel"),
        name="in_proj",
    )(x, g, w, cos, sin)


def _moba_kernel(q_ref, k_ref, v_ref, o_ref, qa_ref, qb_ref, ka_ref, kb_ref, va_ref, vb_ref, *, seq):
    n_blk = seq // MOBA_BLOCK
    q2 = q_ref[...]
    k2 = k_ref[...]
    lane = lax.broadcasted_iota(jnp.int32, (seq, LANES), 1)
    row = lax.broadcasted_iota(jnp.int32, (seq, LANES), 0)
    is_a = lane < HEAD_DIM

    kmean = jnp.mean(k2.astype(F32).reshape(n_blk, MOBA_BLOCK, LANES), axis=1)
    lane8 = lax.broadcasted_iota(jnp.int32, (n_blk, LANES), 1)
    km = jnp.concatenate([jnp.where(lane8 < HEAD_DIM, kmean, 0.0),
                          jnp.where(lane8 >= HEAD_DIM, kmean, 0.0)], axis=0)
    km_hi = km.astype(BF16)
    km_lo = (km - km_hi.astype(F32)).astype(BF16)
    g_all = _dot_nt(jnp.concatenate([km_hi, km_lo], axis=0), q2)
    gates = (g_all[0:n_blk] + g_all[2 * n_blk:3 * n_blk],
             g_all[n_blk:2 * n_blk] + g_all[3 * n_blk:4 * n_blk])

    blk = lax.broadcasted_iota(jnp.int32, (n_blk, seq), 0)
    q_blk = lax.broadcasted_iota(jnp.int32, (n_blk, seq), 1) // MOBA_BLOCK
    past = blk < q_blk
    bias_t = []
    for g in gates:
        g = jnp.where(past, g, -jnp.inf)
        rank = jnp.zeros((n_blk, seq), jnp.int32)
        for jp in range(n_blk):
            gj = g[jp:jp + 1, :]
            beats = (gj > g) | ((gj == g) & (jp < blk))
            rank = rank + beats.astype(jnp.int32)
        keep = ((rank < MOBA_TOPK) & past) | (blk == q_blk)
        bias_t.append(jnp.where(keep, 0.0, NEG))
    pad = jnp.zeros((HEAD_DIM - n_blk, seq), F32)
    bias = jnp.concatenate([bias_t[1], pad, bias_t[0], pad], axis=0).T.astype(BF16)
    key_blk = row // MOBA_BLOCK
    onehot = ((lane == key_blk) | (lane == key_blk + HEAD_DIM)).astype(BF16)

    v2 = v_ref[...]
    ones = jnp.ones_like(v2)
    qa_ref[...] = jnp.where(is_a, q2, bias)
    qb_ref[...] = jnp.where(is_a, bias, q2)
    ka_ref[...] = jnp.where(is_a, k2, onehot)
    kb_ref[...] = jnp.where(is_a, onehot, k2)
    va_ref[...] = jnp.where(is_a, v2, ones)
    vb_ref[...] = jnp.where(is_a, ones, v2)

    tri_r = lax.broadcasted_iota(jnp.int32, (MOBA_BLOCK, MOBA_BLOCK), 0)
    tri_c = lax.broadcasted_iota(jnp.int32, (MOBA_BLOCK, MOBA_BLOCK), 1)
    causal = tri_c <= tri_r
    out_lane = lax.broadcasted_iota(jnp.int32, (MOBA_BLOCK, LANES), 1) < HEAD_DIM

    def rows(j):
        return slice(j * MOBA_BLOCK, (j + 1) * MOBA_BLOCK)

    def half_max(s):
        return jnp.maximum(s[:, :LANES], s[:, LANES:])

    def head(qx_ref, kx_ref, vx_ref, i):
        qx = qx_ref[rows(i), :]
        s_own = jnp.where(causal, _dot_nt(qx, kx_ref[rows(i), :]), NEG)
        m_run = half_max(s_own)
        for j in range(i):
            m_run = jnp.maximum(m_run, half_max(_dot_nt(qx, kx_ref[rows(j), :])))
        m = jnp.max(m_run, axis=1, keepdims=True)
        acc = _dot(jnp.exp(s_own - m).astype(BF16), vx_ref[rows(i), :])
        for j in range(i):
            p = jnp.exp(_dot_nt(qx, kx_ref[rows(j), :]) - m)
            acc = acc + _dot(p.astype(BF16), vx_ref[rows(j), :])
        return acc * (1.0 / pltpu.roll(acc, HEAD_DIM, 1))

    for i in range(n_blk):
        out = jnp.where(out_lane, head(qa_ref, ka_ref, va_ref, i), head(qb_ref, kb_ref, vb_ref, i))
        o_ref[rows(i), :] = out.astype(o_ref.dtype)


def _moba(proj, bsz, seq):
    n = bsz * seq
    qc, kc, vc = COL_QA // LANES, COL_KA // LANES, COL_VA // LANES
    blk = (seq, LANES)
    return pl.pallas_call(
        functools.partial(_moba_kernel, seq=seq),
        grid=(bsz, WIDTH_A // LANES),
        in_specs=[
            pl.BlockSpec(blk, lambda b, h: (b, qc + h)),
            pl.BlockSpec(blk, lambda b, h: (b, kc + h)),
            pl.BlockSpec(blk, lambda b, h: (b, vc + h)),
        ],
        out_specs=pl.BlockSpec(blk, lambda b, h: (b, h)),
        out_shape=jax.ShapeDtypeStruct((n, WIDTH_A), BF16),
        scratch_shapes=[pltpu.VMEM(blk, BF16) for _ in range(6)],
        compiler_params=_params("parallel", "parallel"),
        name="moba",
    )(proj, proj, proj)


def _dilated_kernel(*refs, seq):
    n_groups = len(DIL_DILATIONS)
    in_refs = refs[:3 * n_groups]
    o_ref = refs[3 * n_groups]
    slab_ref, qres_ref, kres_ref, vres_ref, on_ref, ln_ref = refs[3 * n_groups + 1:]
    steps = DIL_STEPS
    n_pairs = WIDTH_BG // LANES
    n_row_blocks = seq // steps

    zeros_pad = jnp.zeros((steps, WIDTH_BG), BF16)
    kres_ref[0:steps, :] = zeros_pad
    vres_ref[0:steps, :] = zeros_pad

    qi = lax.broadcasted_iota(jnp.int32, (steps, 2 * steps), 0)
    kj = lax.broadcasted_iota(jnp.int32, (steps, 2 * steps), 1)
    band = (kj >= qi) & (kj <= qi + steps)
    cur_half = kj >= steps
    is_a = lax.broadcasted_iota(jnp.int32, (steps, LANES), 1) < HEAD_DIM

    for g, d in enumerate(DIL_DILATIONS):
        sub = seq // d
        n_blk = sub // steps
        q_ref, k_ref, v_ref = in_refs[3 * g:3 * g + 3]
        for src, dst, off in ((q_ref, qres_ref, 0), (k_ref, kres_ref, steps), (v_ref, vres_ref, steps)):
            if d == 1:
                dst[off:off + seq, :] = src[...]
                continue
            for hp in range(n_pairs):
                cols = slice(hp * LANES, (hp + 1) * LANES)
                slab_ref[...] = src[:, cols].astype(F32)
                for r in range(d):
                    dst[off + r * sub:off + (r + 1) * sub, cols] = (
                        slab_ref[pl.ds(r, sub, stride=d), :].astype(BF16))

        def row_block(c, _, d=d, n_blk=n_blk, g=g):
            rows_q = pl.ds(pl.multiple_of(c * steps, steps), steps)
            rows_kv = pl.ds(pl.multiple_of(c * steps, steps), 2 * steps)
            prev_valid = (c % n_blk) != 0
            mask = band & (cur_half | prev_valid)
            start = (c % n_blk) * (steps * d) + c // n_blk
            for hp in range(n_pairs):
                cols = slice(hp * LANES, (hp + 1) * LANES)
                q2 = qres_ref[rows_q, cols]
                k2 = kres_ref[rows_kv, cols]
                v2 = vres_ref[rows_kv, cols]
                outs, lses = [], []
                for head_a in (True, False):
                    qm = jnp.where(is_a if head_a else ~is_a, q2, jnp.zeros_like(q2))
                    s = jnp.where(mask, _dot_nt(qm, k2), NEG)
                    m = jnp.max(s, axis=1, keepdims=True)
                    p = jnp.exp(s - m)
                    l = jnp.sum(p, axis=1, keepdims=True)
                    outs.append(_dot(p.astype(BF16), v2) * (1.0 / l))
                    lses.append(jnp.broadcast_to(m + jnp.log(l), (steps, LANES)))
                o2 = jnp.where(is_a, outs[0], outs[1])
                l2 = jnp.where(is_a, lses[0], lses[1])
                slot = g * n_pairs + hp
                if d == 1:
                    on_ref[slot, rows_q, :] = o2
                    ln_ref[slot, rows_q, :] = l2
                else:
                    on_ref[slot, pl.ds(start, steps, stride=d), :] = o2
                    ln_ref[slot, pl.ds(start, steps, stride=d), :] = l2
            return 0

        lax.fori_loop(0, n_row_blocks, row_block, 0, unroll=4)

    for hp in range(n_pairs):
        ls = [ln_ref[g * n_pairs + hp] for g in range(n_groups)]
        mx = functools.reduce(jnp.maximum, ls)
        es = [jnp.exp(l - mx) for l in ls]
        inv = 1.0 / functools.reduce(lambda a, b: a + b, es)
        out = functools.reduce(lambda a, b: a + b,
                               [(es[g] * inv) * on_ref[g * n_pairs + hp] for g in range(n_groups)])
        o_ref[:, hp * LANES:(hp + 1) * LANES] = out.astype(o_ref.dtype)


def _dilated(proj, bsz, seq):
    n = bsz * seq
    n_groups = len(DIL_DILATIONS)
    blk = (seq, WIDTH_BG)
    in_specs = []
    for g in range(n_groups):
        for col in (COL_QB, COL_KB, COL_VB):
            cb = col // WIDTH_BG + g
            in_specs.append(pl.BlockSpec(blk, lambda b, cb=cb: (b, cb)))
    n_slots = n_groups * (WIDTH_BG // LANES)
    return pl.pallas_call(
        functools.partial(_dilated_kernel, seq=seq),
        grid=(bsz,),
        in_specs=in_specs,
        out_specs=pl.BlockSpec(blk, lambda b: (b, 0)),
        out_shape=jax.ShapeDtypeStruct((n, WIDTH_BG), BF16),
        scratch_shapes=[
            pltpu.VMEM((seq, LANES), F32),
            pltpu.VMEM((seq, WIDTH_BG), BF16),
            pltpu.VMEM((seq + DIL_STEPS, WIDTH_BG), BF16),
            pltpu.VMEM((seq + DIL_STEPS, WIDTH_BG), BF16),
            pltpu.VMEM((n_slots, seq, LANES), F32),
            pltpu.VMEM((n_slots, seq, LANES), F32),
        ],
        compiler_params=_params("parallel"),
        name="dilated",
    )(*([proj] * (3 * n_groups)))


def _merge_kernel(x_ref, oa_ref, ob_ref, ga_ref, gb_ref, wa_ref, wb_ref, wo_ref, o_ref):
    ya = _dot(oa_ref[...], wa_ref[...])
    yb = _dot(ob_ref[...], wb_ref[...])
    merged = _sigmoid(ga_ref[...].astype(F32)) * ya + _sigmoid(gb_ref[...].astype(F32)) * yb
    o_ref[...] = x_ref[...] + _dot(merged.astype(BF16), wo_ref[...])


def _merge(x, oa, ob, proj, wa, wb, wo, tm):
    n = x.shape[0]
    const = lambda i: (0, 0)
    return pl.pallas_call(
        _merge_kernel,
        grid=(n // tm,),
        in_specs=[
            pl.BlockSpec((tm, D_MODEL), lambda i: (i, 0)),
            pl.BlockSpec((tm, WIDTH_A), lambda i: (i, 0)),
            pl.BlockSpec((tm, WIDTH_BG), lambda i: (i, 0)),
            pl.BlockSpec((tm, D_MODEL), lambda i: (i, COL_GA // D_MODEL)),
            pl.BlockSpec((tm, D_MODEL), lambda i: (i, COL_GB // D_MODEL)),
            pl.BlockSpec((WIDTH_A, D_MODEL), const),
            pl.BlockSpec((WIDTH_BG, D_MODEL), const),
            pl.BlockSpec((D_MODEL, D_MODEL), const),
        ],
        out_specs=pl.BlockSpec((tm, D_MODEL), lambda i: (i, 0)),
        out_shape=jax.ShapeDtypeStruct((n, D_MODEL), F32),
        compiler_params=_params("parallel"),
        name="merge",
    )(x, oa, ob, proj, proj, wa, wb, wo)


def _swiglu_kernel(x_ref, g_ref, wg_ref, wu_ref, wd_ref, o_ref, h_ref, acc_ref):
    f = pl.program_id(1)

    @pl.when(f == 0)
    def _():
        h_ref[...] = _rms_norm(x_ref[...], g_ref[...]).astype(BF16)
        acc_ref[...] = jnp.zeros_like(acc_ref)

    h = h_ref[...]
    a = _dot(h, wg_ref[...])
    b = _dot(h, wu_ref[...])
    t = (a * _sigmoid(a) * b).astype(BF16)
    acc_ref[...] += _dot(t, wd_ref[...])

    @pl.when(f == pl.num_programs(1) - 1)
    def _():
        o_ref[...] = x_ref[...] + acc_ref[...]


def _swiglu(x, g, wg, wu, wd, tm, tf):
    n = x.shape[0]
    d_ff = wg.shape[1]
    return pl.pallas_call(
        _swiglu_kernel,
        grid=(n // tm, d_ff // tf),
        in_specs=[
            pl.BlockSpec((tm, D_MODEL), lambda i, f: (i, 0)),
            pl.BlockSpec((1, D_MODEL), lambda i, f: (0, 0)),
            pl.BlockSpec((D_MODEL, tf), lambda i, f: (0, f)),
            pl.BlockSpec((D_MODEL, tf), lambda i, f: (0, f)),
            pl.BlockSpec((tf, D_MODEL), lambda i, f: (f, 0)),
        ],
        out_specs=pl.BlockSpec((tm, D_MODEL), lambda i, f: (i, 0)),
        out_shape=jax.ShapeDtypeStruct((n, D_MODEL), F32),
        scratch_shapes=[pltpu.VMEM((tm, D_MODEL), BF16), pltpu.VMEM((tm, D_MODEL), F32)],
        compiler_params=_params("parallel", "arbitrary"),
        name="swiglu",
    )(x, g, wg, wu, wd)


MOE_TM = 1024
MOE_TG = 256
MOE_TC = 512
MOE_TF = 896
META_E1, META_E2, META_R1, META_R2, META_W1, META_W2 = range(6)


def _route_kernel(x_ref, g_ref, wr_ref, h_ref, meta_ref, metat_ref, cs_ref, tot_ref, carry_ref):
    @pl.when(pl.program_id(0) == 0)
    def _():
        carry_ref[...] = jnp.zeros_like(carry_ref)

    h = _rms_norm(x_ref[...], g_ref[...])
    h_ref[...] = h.astype(BF16)
    logits = jnp.dot(h, wr_ref[...], preferred_element_type=F32, precision=lax.Precision.HIGHEST)
    tc = logits.shape[0]
    lane = lax.broadcasted_iota(jnp.int32, logits.shape, 1)
    logits = jnp.where(lane < N_EXPERTS, logits, -jnp.inf)
    v1 = jnp.max(logits, axis=1, keepdims=True)
    i1 = jnp.min(jnp.where(logits == v1, lane, LANES), axis=1, keepdims=True)
    rest = jnp.where(lane == i1, -jnp.inf, logits)
    v2 = jnp.max(rest, axis=1, keepdims=True)
    i2 = jnp.min(jnp.where(rest == v2, lane, LANES), axis=1, keepdims=True)
    e2 = jnp.exp(v2 - v1)
    w1 = 1.0 / (1.0 + e2)
    w2 = e2 / (1.0 + e2)

    chosen = jnp.where((lane == i1) | (lane == i2), 1.0, 0.0)
    r_i = lax.broadcasted_iota(jnp.int32, (tc, tc), 0)
    c_i = lax.broadcasted_iota(jnp.int32, (tc, tc), 1)
    before = jnp.where(c_i < r_i, 1.0, 0.0).astype(BF16)
    carry = carry_ref[0:1, :]
    rank = _dot(before, chosen.astype(BF16)) + carry
    r1 = jnp.sum(jnp.where(lane == i1, rank, 0.0), axis=1, keepdims=True)
    r2 = jnp.sum(jnp.where(lane == i2, rank, 0.0), axis=1, keepdims=True)
    cols = (i1.astype(F32), i2.astype(F32), r1, r2, w1, w2)
    meta = jnp.zeros(logits.shape, F32)
    for idx, col in enumerate(cols):
        meta = jnp.where(lane == idx, col, meta)
    meta_ref[...] = meta
    metat_ref[...] = meta.T[0:8, :]
    cs_ref[0] = jnp.broadcast_to(carry, (8, LANES))
    total = carry + jnp.sum(chosen, axis=0, keepdims=True)
    carry_ref[...] = jnp.broadcast_to(total, (8, LANES))
    tot_ref[...] = jnp.broadcast_to(total, (8, LANES))


def _route(x, g, wr):
    n = x.shape[0]
    tc = MOE_TC
    n_c = n // tc
    return pl.pallas_call(
        _route_kernel,
        grid=(n_c,),
        in_specs=[
            pl.BlockSpec((tc, D_MODEL), lambda c: (c, 0)),
            pl.BlockSpec((1, D_MODEL), lambda c: (0, 0)),
            pl.BlockSpec((D_MODEL, LANES), lambda c: (0, 0)),
        ],
        out_specs=[
            pl.BlockSpec((tc, D_MODEL), lambda c: (c, 0)),
            pl.BlockSpec((tc, LANES), lambda c: (c, 0)),
            pl.BlockSpec((8, tc), lambda c: (0, c)),
            pl.BlockSpec((1, 8, LANES), lambda c: (c, 0, 0)),
            pl.BlockSpec((8, LANES), lambda c: (0, 0)),
        ],
        out_shape=[
            jax.ShapeDtypeStruct((n, D_MODEL), BF16),
            jax.ShapeDtypeStruct((n, LANES), F32),
            jax.ShapeDtypeStruct((8, n), F32),
            jax.ShapeDtypeStruct((n_c, 8, LANES), F32),
            jax.ShapeDtypeStruct((8, LANES), F32),
        ],
        scratch_shapes=[pltpu.VMEM((8, LANES), F32)],
        compiler_params=_params("arbitrary"),
        name="route",
    )(x, g, wr)


def _moe_plan(cs, tot, n):
    tm, tg, tc = MOE_TM, MOE_TG, MOE_TC
    n_c = n // tc
    n_tiles = 2 * n // tm + N_EXPERTS
    n_g = n_tiles * tm // tg
    k_max = n_g + N_EXPERTS * n_c
    i32 = jnp.int32
    counts = tot[0, :N_EXPERTS].astype(i32)
    cs = cs[:, 0, :N_EXPERTS].astype(i32)
    padded = (counts + tm - 1) // tm * tm
    ends = jnp.cumsum(padded)
    off = ends - padded
    tile_start = jnp.arange(n_tiles, dtype=i32) * tm
    tile_e = jnp.minimum(jnp.sum(tile_start[:, None] >= ends[None, :], axis=1), N_EXPERTS - 1).astype(i32)
    n_valid_tiles = (ends[-1] // tm).astype(i32).reshape(1)

    p0 = jnp.arange(n_g, dtype=i32) * tg
    g_e = tile_e[p0 // tm]
    lo = p0 - off[g_e]
    cs_g = cs[:, g_e].T
    c_lo = jnp.maximum(jnp.sum(cs_g <= lo[:, None], axis=1) - 1, 0)
    c_hi = jnp.maximum(jnp.sum(cs_g < (lo + tg)[:, None], axis=1) - 1, c_lo)
    n_it = c_hi - c_lo + 1
    it_end = jnp.cumsum(n_it)
    it_start = it_end - n_it
    n_items = it_end[-1].astype(i32).reshape(1)
    k = jnp.arange(k_max, dtype=i32)
    valid = k < n_items[0]
    item_g = jnp.minimum(jnp.sum(k[:, None] >= it_end[None, :], axis=1), n_g - 1).astype(i32)
    item_c = jnp.minimum(c_lo[item_g] + k - it_start[item_g], c_hi[item_g]).astype(i32)
    item_first = ((k == it_start[item_g]) & valid).astype(i32)

    order = jnp.argsort(jnp.where(valid, item_c * n_g + item_g, n_c * n_g + k))
    last_valid = n_items[0] - 1
    cc = item_c[order]
    cg = item_g[order]
    cc = jnp.where(valid, cc, cc[last_valid]).astype(i32)
    cg = jnp.where(valid, cg, cg[last_valid]).astype(i32)
    prev_c = jnp.concatenate([jnp.full((1,), -1, i32), cc[:-1]])
    next_c = jnp.concatenate([cc[1:], jnp.full((1,), -1, i32)])
    comb_first = (valid & (cc != prev_c)).astype(i32)
    comb_last = (valid & ((cc != next_c) | (k == last_valid))).astype(i32)
    return dict(tile_e=tile_e, n_valid_tiles=n_valid_tiles, g_e=g_e.astype(i32), off=off.astype(i32),
                n_items=n_items, item_g=item_g, item_c=item_c, item_first=item_first,
                comb_c=cc, comb_g=cg, comb_first=comb_first, comb_last=comb_last,
                n_tiles=n_tiles, n_g=n_g, k_max=k_max)


def _gather_kernel(ig_ref, ic_ref, first_ref, n_ref, ge_ref, off_ref, h_ref, mt_ref, o_ref):
    k = pl.program_id(0)
    tg, tc = o_ref.shape[0], h_ref.shape[0]

    @pl.when(k < n_ref[0])
    def _():
        g = ig_ref[k]
        e = ge_ref[g]
        base = (off_ref[e] - g * tg).astype(F32)
        ef = e.astype(F32)
        mt = mt_ref[...]
        tgt1 = jnp.where(mt[META_E1:META_E1 + 1] == ef, mt[META_R1:META_R1 + 1] + base, -1.0)
        tgt2 = jnp.where(mt[META_E2:META_E2 + 1] == ef, mt[META_R2:META_R2 + 1] + base, -1.0)
        row = lax.broadcasted_iota(jnp.int32, (tg, tc), 0).astype(F32)
        pick = jnp.where((row == tgt1) | (row == tgt2), 1.0, 0.0).astype(BF16)
        part = _dot(pick, h_ref[...]).astype(BF16)

        @pl.when(first_ref[k] == 1)
        def _():
            o_ref[...] = part

        @pl.when(first_ref[k] == 0)
        def _():
            o_ref[...] += part


def _gather(plan, h, metat):
    tg, tc = MOE_TG, MOE_TC
    grid_spec = pltpu.PrefetchScalarGridSpec(
        num_scalar_prefetch=6,
        grid=(plan["k_max"],),
        in_specs=[
            pl.BlockSpec((tc, D_MODEL), lambda k, ig, ic, *_: (ic[k], 0)),
            pl.BlockSpec((8, tc), lambda k, ig, ic, *_: (0, ic[k])),
        ],
        out_specs=pl.BlockSpec((tg, D_MODEL), lambda k, ig, ic, *_: (ig[k], 0)),
    )
    return pl.pallas_call(
        _gather_kernel,
        grid_spec=grid_spec,
        out_shape=jax.ShapeDtypeStruct((plan["n_g"] * tg, D_MODEL), BF16),
        compiler_params=_params("arbitrary"),
        name="moe_gather",
    )(plan["item_g"], plan["item_c"], plan["item_first"], plan["n_items"], plan["g_e"], plan["off"], h, metat)


def _ffn_kernel(te_ref, nv_ref, xs_ref, wg_ref, wu_ref, wd_ref, o_ref, acc_ref):
    i = pl.program_id(0)
    f = pl.program_id(1)
    valid = i < nv_ref[0]

    @pl.when(valid & (f == 0))
    def _():
        acc_ref[...] = jnp.zeros_like(acc_ref)

    @pl.when(valid)
    def _():
        xs = xs_ref[...]
        a = _dot(xs, wg_ref[0])
        b = _dot(xs, wu_ref[0])
        t = (a * _sigmoid(a) * b).astype(BF16)
        acc_ref[...] += _dot(t, wd_ref[0])

    last = f == pl.num_programs(1) - 1

    @pl.when(valid & last)
    def _():
        o_ref[...] = acc_ref[...].astype(o_ref.dtype)

    @pl.when(jnp.logical_not(valid) & last)
    def _():
        o_ref[...] = jnp.zeros_like(o_ref)


def _ffn(plan, xs, wg, wu, wd):
    tm, tf = MOE_TM, MOE_TF
    d_ff = wg.shape[2]
    n_f = d_ff // tf

    def f_eff(i, f, nv):
        return jnp.where(i < nv[0], f, n_f - 1)

    grid_spec = pltpu.PrefetchScalarGridSpec(
        num_scalar_prefetch=2,
        grid=(plan["n_tiles"], n_f),
        in_specs=[
            pl.BlockSpec((tm, D_MODEL), lambda i, f, te, nv: (i, 0)),
            pl.BlockSpec((1, D_MODEL, tf), lambda i, f, te, nv: (te[i], 0, f_eff(i, f, nv))),
            pl.BlockSpec((1, D_MODEL, tf), lambda i, f, te, nv: (te[i], 0, f_eff(i, f, nv))),
            pl.BlockSpec((1, tf, D_MODEL), lambda i, f, te, nv: (te[i], f_eff(i, f, nv), 0)),
        ],
        out_specs=pl.BlockSpec((tm, D_MODEL), lambda i, f, te, nv: (i, 0)),
        scratch_shapes=[pltpu.VMEM((tm, D_MODEL), F32)],
    )
    return pl.pallas_call(
        _ffn_kernel,
        grid_spec=grid_spec,
        out_shape=jax.ShapeDtypeStruct(xs.shape, BF16),
        compiler_params=_params("arbitrary", "arbitrary"),
        name="moe_ffn",
    )(plan["tile_e"], plan["n_valid_tiles"], xs, wg, wu, wd)


def _combine_kernel(cc_ref, cg_ref, first_ref, last_ref, n_ref, ge_ref, off_ref,
                    x_ref, meta_ref, y_ref, gf_ref, o_ref, acc_ref):
    k = pl.program_id(0)
    tg, tc = y_ref.shape[0], x_ref.shape[0]

    @pl.when(k < n_ref[0])
    def _():
        g = cg_ref[k]
        e = ge_ref[g]
        base = (off_ref[e] - g * tg).astype(F32)
        ef = e.astype(F32)
        meta = meta_ref[...]
        col = lambda c: meta[:, c:c + 1]
        tgt1 = jnp.where(col(META_E1) == ef, col(META_R1) + base, -1.0)
        tgt2 = jnp.where(col(META_E2) == ef, col(META_R2) + base, -1.0)
        pos = lax.broadcasted_iota(jnp.int32, (tc, tg), 1).astype(F32)
        spread = (jnp.where(pos == tgt1, col(META_W1), 0.0) + jnp.where(pos == tgt2, col(META_W2), 0.0))
        part = _dot(spread.astype(BF16), y_ref[...])

        @pl.when(first_ref[k] == 1)
        def _():
            acc_ref[...] = part

        @pl.when(first_ref[k] == 0)
        def _():
            acc_ref[...] += part

        @pl.when(last_ref[k] == 1)
        def _():
            o_ref[...] = _rms_norm(x_ref[...] + acc_ref[...], gf_ref[...])


def _combine(plan, x, meta, y, g_final):
    n = x.shape[0]
    tg, tc = MOE_TG, MOE_TC
    grid_spec = pltpu.PrefetchScalarGridSpec(
        num_scalar_prefetch=7,
        grid=(plan["k_max"],),
        in_specs=[
            pl.BlockSpec((tc, D_MODEL), lambda k, cc, cg, *_: (cc[k], 0)),
            pl.BlockSpec((tc, LANES), lambda k, cc, cg, *_: (cc[k], 0)),
            pl.BlockSpec((tg, D_MODEL), lambda k, cc, cg, *_: (cg[k], 0)),
            pl.BlockSpec((1, D_MODEL), lambda k, cc, cg, *_: (0, 0)),
        ],
        out_specs=pl.BlockSpec((tc, D_MODEL), lambda k, cc, cg, *_: (cc[k], 0)),
        scratch_shapes=[pltpu.VMEM((tc, D_MODEL), F32)],
    )
    return pl.pallas_call(
        _combine_kernel,
        grid_spec=grid_spec,
        out_shape=jax.ShapeDtypeStruct((n, D_MODEL), F32),
        compiler_params=_params("arbitrary"),
        name="moe_combine",
    )(plan["comb_c"], plan["comb_g"], plan["comb_first"], plan["comb_last"], plan["n_items"],
      plan["g_e"], plan["off"], x, meta, y, g_final)


def _moe(x, g, wr, wg, wu, wd, g_final):
    n = x.shape[0]
    h, meta, metat, cs, tot = _route(x, g, wr)
    plan = _moe_plan(cs, tot, n)
    xs = _gather(plan, h, metat)
    y = _ffn(plan, xs, wg, wu, wd)
    return _combine(plan, x, meta, y, g_final)


def _prep_w_in(w_in):
    sizes = [WIDTH_A] * 3 + [WIDTH_B] * 3 + [D_MODEL] * 2
    qa, ka, va, qb, kb, vb, ga, gb = jnp.split(w_in, np.cumsum(sizes)[:-1].tolist(), axis=1)
    scale = HEAD_DIM ** -0.5
    return jnp.concatenate([ga, gb, qa * scale, ka, va, qb * scale, kb, vb], axis=1).astype(BF16)


def _rope_tables(seq):
    half = HEAD_DIM // 2
    inv = ROPE_THETA ** (-jnp.arange(half, dtype=F32) / half)
    ang = jnp.arange(seq, dtype=F32)[:, None] * inv[None, :]
    cos, sin = jnp.cos(ang), jnp.sin(ang)
    n_heads = WIDTH_B // HEAD_DIM
    return (jnp.tile(jnp.concatenate([cos, cos], axis=1), (1, n_heads)),
            jnp.tile(jnp.concatenate([-sin, sin], axis=1), (1, n_heads)))


def kernel(x, norm_mix_0, w_in_0, w_proj_a_0, w_proj_b_0, w_out_0, norm_ffn_0, w_gate_0, w_up_0, w_down_0,
           norm_mix_1, w_in_1, w_proj_a_1, w_proj_b_1, w_out_1, norm_ffn_1, w_router_1, w_gate_e_1, w_up_e_1,
           w_down_e_1, norm_final):
    bsz, seq, _ = x.shape
    assert seq % MOBA_BLOCK == 0 and seq % (DIL_STEPS * max(DIL_DILATIONS)) == 0
    cos, sin = _rope_tables(seq)
    row = lambda v: v.reshape(1, D_MODEL).astype(F32)
    xf = x.reshape(bsz * seq, D_MODEL)

    def mixer(xf, nm, w_in, wa, wb, wo):
        proj = _in_proj(xf, row(nm), _prep_w_in(w_in), cos, sin, tm=512)
        oa = _moba(proj, bsz, seq)
        ob = _dilated(proj, bsz, seq)
        return _merge(xf, oa, ob, proj, wa.astype(BF16), wb.astype(BF16), wo.astype(BF16), tm=512)

    xf = mixer(xf, norm_mix_0, w_in_0, w_proj_a_0, w_proj_b_0, w_out_0)
    xf = _swiglu(xf, row(norm_ffn_0), w_gate_0.astype(BF16), w_up_0.astype(BF16), w_down_0.astype(BF16),
                 tm=512, tf=1408)
    xf = mixer(xf, norm_mix_1, w_in_1, w_proj_a_1, w_proj_b_1, w_out_1)
    wr = jnp.pad(w_router_1.astype(F32), ((0, 0), (0, LANES - N_EXPERTS)))
    xf = _moe(xf, row(norm_ffn_1), wr, w_gate_e_1.astype(BF16), w_up_e_1.astype(BF16),
              w_down_e_1.astype(BF16), row(norm_final))
    return xf.reshape(bsz, seq, D_MODEL)
```

```python
import functools

import jax
import jax.numpy as jnp
import numpy as np
from jax import lax
from jax.experimental import pallas as pl
from jax.experimental.pallas import tpu as pltpu

D_MODEL = 1024
HEAD_DIM = 64
MOBA_HEADS = 8
MOBA_BLOCK = 256
MOBA_TOPK = 3
DIL_DILATIONS = (1, 4, 16)
DIL_STEPS = 128
DIL_HEADS_PER_GROUP = 4
ROPE_THETA = 10000.0
RMS_EPS = 1e-6
N_EXPERTS = 8

WIDTH_A = MOBA_HEADS * HEAD_DIM
WIDTH_BG = DIL_HEADS_PER_GROUP * HEAD_DIM
WIDTH_B = len(DIL_DILATIONS) * WIDTH_BG

LANES = 128
NEG = -1e30
VMEM_LIMIT = 56 * 1024 * 1024

COL_GA = 0
COL_GB = COL_GA + D_MODEL
COL_QA = COL_GB + D_MODEL
COL_KA = COL_QA + WIDTH_A
COL_VA = COL_KA + WIDTH_A
COL_QB = COL_VA + WIDTH_A
COL_KB = COL_QB + WIDTH_B
COL_VB = COL_KB + WIDTH_B
PROJ_COLS = COL_VB + WIDTH_B

F32 = jnp.float32
BF16 = jnp.bfloat16


def _params(*sem):
    return pltpu.CompilerParams(dimension_semantics=sem, vmem_limit_bytes=VMEM_LIMIT)


def _dot(a, b):
    return jnp.dot(a, b, preferred_element_type=F32)


def _dot_nt(a, b):
    return lax.dot_general(a, b, (((1,), (1,)), ((), ())), preferred_element_type=F32)


def _rms_norm(x, g):
    return x * lax.rsqrt(jnp.mean(x * x, axis=-1, keepdims=True) + RMS_EPS) * g


def _sigmoid(z):
    return 1.0 / (1.0 + jnp.exp(-z))


def _rope_apply(acc, cos, sin_signed):
    width = acc.shape[-1]
    lane = lax.broadcasted_iota(jnp.int32, acc.shape, 1)
    first_half = (lane % HEAD_DIM) < (HEAD_DIM // 2)
    rot = jnp.where(first_half,
                    pltpu.roll(acc, width - HEAD_DIM // 2, 1),
                    pltpu.roll(acc, HEAD_DIM // 2, 1))
    return acc * cos + rot * sin_signed


def _in_proj_kernel(x_ref, g_ref, w_ref, cos_ref, sin_ref, o_ref):
    h = _rms_norm(x_ref[...], g_ref[...]).astype(BF16)
    plain = ((COL_GA, D_MODEL), (COL_GB, D_MODEL), (COL_VA, WIDTH_A), (COL_VB, WIDTH_B))
    rotary = ((COL_QA, WIDTH_A), (COL_KA, WIDTH_A), (COL_QB, WIDTH_B), (COL_KB, WIDTH_B))
    for c0, w in plain:
        o_ref[:, c0:c0 + w] = _dot(h, w_ref[:, c0:c0 + w]).astype(BF16)
    for c0, w in rotary:
        acc = _dot(h, w_ref[:, c0:c0 + w])
        o_ref[:, c0:c0 + w] = _rope_apply(acc, cos_ref[:, :w], sin_ref[:, :w]).astype(BF16)


def _in_proj(x, g, w, cos, sin, tm):
    n = x.shape[0]
    seq = cos.shape[0]
    n_seq_tiles = seq // tm
    return pl.pallas_call(
        _in_proj_kernel,
        grid=(n // tm,),
        in_specs=[
            pl.BlockSpec((tm, D_MODEL), lambda i: (i, 0)),
            pl.BlockSpec((1, D_MODEL), lambda i: (0, 0)),
            pl.BlockSpec((D_MODEL, PROJ_COLS), lambda i: (0, 0), pipeline_mode=pl.Buffered(1)),
            pl.BlockSpec((tm, WIDTH_B), lambda i: (i % n_seq_tiles, 0)),
            pl.BlockSpec((tm, WIDTH_B), lambda i: (i % n_seq_tiles, 0)),
        ],
        out_specs=pl.BlockSpec((tm, PROJ_COLS), lambda i: (i, 0)),
        out_shape=jax.ShapeDtypeStruct((n, PROJ_COLS), BF16),
        compiler_params=_params("parallel"),
        name="in_proj",
    )(x, g, w, cos, sin)


def _moba_kernel(q_ref, k_ref, v_ref, o_ref, qa_ref, qb_ref, ka_ref, kb_ref, va_ref, vb_ref, *, seq):
    n_blk = seq // MOBA_BLOCK
    q2 = q_ref[...]
    k2 = k_ref[...]
    lane = lax.broadcasted_iota(jnp.int32, (seq, LANES), 1)
    row = lax.broadcasted_iota(jnp.int32, (seq, LANES), 0)
    is_a = lane < HEAD_DIM

    kmean = jnp.mean(k2.astype(F32).reshape(n_blk, MOBA_BLOCK, LANES), axis=1)
    lane8 = lax.broadcasted_iota(jnp.int32, (n_blk, LANES), 1)
    km = jnp.concatenate([jnp.where(lane8 < HEAD_DIM, kmean, 0.0),
                          jnp.where(lane8 >= HEAD_DIM, kmean, 0.0)], axis=0)
    km_hi = km.astype(BF16)
    km_lo = (km - km_hi.astype(F32)).astype(BF16)
    g_all = _dot_nt(jnp.concatenate([km_hi, km_lo], axis=0), q2)
    gates = (g_all[0:n_blk] + g_all[2 * n_blk:3 * n_blk],
             g_all[n_blk:2 * n_blk] + g_all[3 * n_blk:4 * n_blk])

    blk = lax.broadcasted_iota(jnp.int32, (n_blk, seq), 0)
    q_blk = lax.broadcasted_iota(jnp.int32, (n_blk, seq), 1) // MOBA_BLOCK
    past = blk < q_blk
    bias_t = []
    for g in gates:
        g = jnp.where(past, g, -jnp.inf)
        rank = jnp.zeros((n_blk, seq), jnp.int32)
        for jp in range(n_blk):
            gj = g[jp:jp + 1, :]
            beats = (gj > g) | ((gj == g) & (jp < blk))
            rank = rank + beats.astype(jnp.int32)
        keep = ((rank < MOBA_TOPK) & past) | (blk == q_blk)
        bias_t.append(jnp.where(keep, 0.0, NEG))
    pad = jnp.zeros((HEAD_DIM - n_blk, seq), F32)
    bias = jnp.concatenate([bias_t[1], pad, bias_t[0], pad], axis=0).T.astype(BF16)
    key_blk = row // MOBA_BLOCK
    onehot = ((lane == key_blk) | (lane == key_blk + HEAD_DIM)).astype(BF16)

    v2 = v_ref[...]
    ones = jnp.ones_like(v2)
    qa_ref[...] = jnp.where(is_a, q2, bias)
    qb_ref[...] = jnp.where(is_a, bias, q2)
    ka_ref[...] = jnp.where(is_a, k2, onehot)
    kb_ref[...] = jnp.where(is_a, onehot, k2)
    va_ref[...] = jnp.where(is_a, v2, ones)
    vb_ref[...] = jnp.where(is_a, ones, v2)

    tri_r = lax.broadcasted_iota(jnp.int32, (MOBA_BLOCK, MOBA_BLOCK), 0)
    tri_c = lax.broadcasted_iota(jnp.int32, (MOBA_BLOCK, MOBA_BLOCK), 1)
    causal = tri_c <= tri_r
    out_lane = lax.broadcasted_iota(jnp.int32, (MOBA_BLOCK, LANES), 1) < HEAD_DIM

    def rows(j):
        return slice(j * MOBA_BLOCK, (j + 1) * MOBA_BLOCK)

    def half_max(s):
        return jnp.maximum(s[:, :LANES], s[:, LANES:])

    def head(qx_ref, kx_ref, vx_ref, i):
        qx = qx_ref[rows(i), :]
        s_own = jnp.where(causal, _dot_nt(qx, kx_ref[rows(i), :]), NEG)
        m_run = half_max(s_own)
        for j in range(i):
            m_run = jnp.maximum(m_run, half_max(_dot_nt(qx, kx_ref[rows(j), :])))
        m = jnp.max(m_run, axis=1, keepdims=True)
        acc = _dot(jnp.exp(s_own - m).astype(BF16), vx_ref[rows(i), :])
        for j in range(i):
            p = jnp.exp(_dot_nt(qx, kx_ref[rows(j), :]) - m)
            acc = acc + _dot(p.astype(BF16), vx_ref[rows(j), :])
        return acc * (1.0 / pltpu.roll(acc, HEAD_DIM, 1))

    for i in range(n_blk):
        out = jnp.where(out_lane, head(qa_ref, ka_ref, va_ref, i), head(qb_ref, kb_ref, vb_ref, i))
        o_ref[rows(i), :] = out.astype(o_ref.dtype)


def _moba(proj, bsz, seq):
    n = bsz * seq
    qc, kc, vc = COL_QA // LANES, COL_KA // LANES, COL_VA // LANES
    blk = (seq, LANES)
    return pl.pallas_call(
        functools.partial(_moba_kernel, seq=seq),
        grid=(bsz, WIDTH_A // LANES),
        in_specs=[
            pl.BlockSpec(blk, lambda b, h: (b, qc + h)),
            pl.BlockSpec(blk, lambda b, h: (b, kc + h)),
            pl.BlockSpec(blk, lambda b, h: (b, vc + h)),
        ],
        out_specs=pl.BlockSpec(blk, lambda b, h: (b, h)),
        out_shape=jax.ShapeDtypeStruct((n, WIDTH_A), BF16),
        scratch_shapes=[pltpu.VMEM(blk, BF16) for _ in range(6)],
        compiler_params=_params("parallel", "parallel"),
        name="moba",
    )(proj, proj, proj)


def _dilated_kernel(*refs, seq):
    n_groups = len(DIL_DILATIONS)
    in_refs = refs[:3 * n_groups]
    o_ref = refs[3 * n_groups]
    slab_ref, qres_ref, kres_ref, vres_ref, on_ref, ln_ref = refs[3 * n_groups + 1:]
    steps = DIL_STEPS
    n_pairs = WIDTH_BG // LANES
    n_row_blocks = seq // steps

    zeros_pad = jnp.zeros((steps, WIDTH_BG), BF16)
    kres_ref[0:steps, :] = zeros_pad
    vres_ref[0:steps, :] = zeros_pad

    qi = lax.broadcasted_iota(jnp.int32, (steps, 2 * steps), 0)
    kj = lax.broadcasted_iota(jnp.int32, (steps, 2 * steps), 1)
    band = (kj >= qi) & (kj <= qi + steps)
    cur_half = kj >= steps
    is_a = lax.broadcasted_iota(jnp.int32, (steps, LANES), 1) < HEAD_DIM

    for g, d in enumerate(DIL_DILATIONS):
        sub = seq // d
        n_blk = sub // steps
        q_ref, k_ref, v_ref = in_refs[3 * g:3 * g + 3]
        for src, dst, off in ((q_ref, qres_ref, 0), (k_ref, kres_ref, steps), (v_ref, vres_ref, steps)):
            if d == 1:
                dst[off:off + seq, :] = src[...]
                continue
            for hp in range(n_pairs):
                cols = slice(hp * LANES, (hp + 1) * LANES)
                slab_ref[...] = src[:, cols].astype(F32)
                for r in range(d):
                    dst[off + r * sub:off + (r + 1) * sub, cols] = (
                        slab_ref[pl.ds(r, sub, stride=d), :].astype(BF16))

        def row_block(c, _, d=d, n_blk=n_blk, g=g):
            rows_q = pl.ds(pl.multiple_of(c * steps, steps), steps)
            rows_kv = pl.ds(pl.multiple_of(c * steps, steps), 2 * steps)
            prev_valid = (c % n_blk) != 0
            mask = band & (cur_half | prev_valid)
            start = (c % n_blk) * (steps * d) + c // n_blk
            for hp in range(n_pairs):
                cols = slice(hp * LANES, (hp + 1) * LANES)
                q2 = qres_ref[rows_q, cols]
                k2 = kres_ref[rows_kv, cols]
                v2 = vres_ref[rows_kv, cols]
                outs, lses = [], []
                for head_a in (True, False):
                    qm = jnp.where(is_a if head_a else ~is_a, q2, jnp.zeros_like(q2))
                    s = jnp.where(mask, _dot_nt(qm, k2), NEG)
                    m = jnp.max(s, axis=1, keepdims=True)
                    p = jnp.exp(s - m)
                    l = jnp.sum(p, axis=1, keepdims=True)
                    outs.append(_dot(p.astype(BF16), v2) * (1.0 / l))
                    lses.append(jnp.broadcast_to(m + jnp.log(l), (steps, LANES)))
                o2 = jnp.where(is_a, outs[0], outs[1])
                l2 = jnp.where(is_a, lses[0], lses[1])
                slot = g * n_pairs + hp
                if d == 1:
                    on_ref[slot, rows_q, :] = o2
                    ln_ref[slot, rows_q, :] = l2
                else:
                    on_ref[slot, pl.ds(start, steps, stride=d), :] = o2
                    ln_ref[slot, pl.ds(start, steps, stride=d), :] = l2
            return 0

        lax.fori_loop(0, n_row_blocks, row_block, 0, unroll=4)

    for hp in range(n_pairs):
        ls = [ln_ref[g * n_pairs + hp] for g in range(n_groups)]
        mx = functools.reduce(jnp.maximum, ls)
        es = [jnp.exp(l - mx) for l in ls]
        inv = 1.0 / functools.reduce(lambda a, b: a + b, es)
        out = functools.reduce(lambda a, b: a + b,
                               [(es[g] * inv) * on_ref[g * n_pairs + hp] for g in range(n_groups)])
        o_ref[:, hp * LANES:(hp + 1) * LANES] = out.astype(o_ref.dtype)


def _dilated(proj, bsz, seq):
    n = bsz * seq
    n_groups = len(DIL_DILATIONS)
    blk = (seq, WIDTH_BG)
    in_specs = []
    for g in range(n_groups):
        for col in (COL_QB, COL_KB, COL_VB):
            cb = col // WIDTH_BG + g
            in_specs.append(pl.BlockSpec(blk, lambda b, cb=cb: (b, cb)))
    n_slots = n_groups * (WIDTH_BG // LANES)
    return pl.pallas_call(
        functools.partial(_dilated_kernel, seq=seq),
        grid=(bsz,),
        in_specs=in_specs,
        out_specs=pl.BlockSpec(blk, lambda b: (b, 0)),
        out_shape=jax.ShapeDtypeStruct((n, WIDTH_BG), BF16),
        scratch_shapes=[
            pltpu.VMEM((seq, LANES), F32),
            pltpu.VMEM((seq, WIDTH_BG), BF16),
            pltpu.VMEM((seq + DIL_STEPS, WIDTH_BG), BF16),
            pltpu.VMEM((seq + DIL_STEPS, WIDTH_BG), BF16),
            pltpu.VMEM((n_slots, seq, LANES), F32),
            pltpu.VMEM((n_slots, seq, LANES), F32),
        ],
        compiler_params=_params("parallel"),
        name="dilated",
    )(*([proj] * (3 * n_groups)))


def _merge_kernel(x_ref, oa_ref, ob_ref, ga_ref, gb_ref, wa_ref, wb_ref, wo_ref, o_ref):
    ya = _dot(oa_ref[...], wa_ref[...])
    yb = _dot(ob_ref[...], wb_ref[...])
    merged = _sigmoid(ga_ref[...].astype(F32)) * ya + _sigmoid(gb_ref[...].astype(F32)) * yb
    o_ref[...] = x_ref[...] + _dot(merged.astype(BF16), wo_ref[...])


def _merge(x, oa, ob, proj, wa, wb, wo, tm):
    n = x.shape[0]
    const = lambda i: (0, 0)
    return pl.pallas_call(
        _merge_kernel,
        grid=(n // tm,),
        in_specs=[
            pl.BlockSpec((tm, D_MODEL), lambda i: (i, 0)),
            pl.BlockSpec((tm, WIDTH_A), lambda i: (i, 0)),
            pl.BlockSpec((tm, WIDTH_BG), lambda i: (i, 0)),
            pl.BlockSpec((tm, D_MODEL), lambda i: (i, COL_GA // D_MODEL)),
            pl.BlockSpec((tm, D_MODEL), lambda i: (i, COL_GB // D_MODEL)),
            pl.BlockSpec((WIDTH_A, D_MODEL), const),
            pl.BlockSpec((WIDTH_BG, D_MODEL), const),
            pl.BlockSpec((D_MODEL, D_MODEL), const),
        ],
        out_specs=pl.BlockSpec((tm, D_MODEL), lambda i: (i, 0)),
        out_shape=jax.ShapeDtypeStruct((n, D_MODEL), F32),
        compiler_params=_params("parallel"),
        name="merge",
    )(x, oa, ob, proj, proj, wa, wb, wo)


def _swiglu_kernel(x_ref, g_ref, wg_ref, wu_ref, wd_ref, o_ref, *, chunk):
    x = x_ref[...]
    h = _rms_norm(x, g_ref[...]).astype(BF16)
    d_ff = wg_ref.shape[1]
    acc = x
    for c0 in range(0, d_ff, chunk):
        cols = slice(c0, min(c0 + chunk, d_ff))
        a = _dot(h, wg_ref[:, cols])
        b = _dot(h, wu_ref[:, cols])
        t = (a * _sigmoid(a) * b).astype(BF16)
        acc = acc + _dot(t, wd_ref[cols, :])
    o_ref[...] = acc


def _swiglu(x, g, wg, wu, wd, tm, chunk):
    n = x.shape[0]
    d_ff = wg.shape[1]
    const = lambda i: (0, 0)
    return pl.pallas_call(
        functools.partial(_swiglu_kernel, chunk=chunk),
        grid=(n // tm,),
        in_specs=[
            pl.BlockSpec((tm, D_MODEL), lambda i: (i, 0)),
            pl.BlockSpec((1, D_MODEL), const),
            pl.BlockSpec((D_MODEL, d_ff), const, pipeline_mode=pl.Buffered(1)),
            pl.BlockSpec((D_MODEL, d_ff), const, pipeline_mode=pl.Buffered(1)),
            pl.BlockSpec((d_ff, D_MODEL), const, pipeline_mode=pl.Buffered(1)),
        ],
        out_specs=pl.BlockSpec((tm, D_MODEL), lambda i: (i, 0)),
        out_shape=jax.ShapeDtypeStruct((n, D_MODEL), F32),
        compiler_params=_params("parallel"),
        name="swiglu",
    )(x, g, wg, wu, wd)


MOE_TM = 1024
MOE_TG = 256
MOE_TC = 512
MOE_SLOTS = 3
MOE_TF = 512
META_E1, META_E2, META_R1, META_R2, META_W1, META_W2 = range(6)


def _route_kernel(x_ref, g_ref, wr_ref, h_ref, meta_ref, metat_ref, cs_ref, tot_ref, carry_ref):
    @pl.when(pl.program_id(0) == 0)
    def _():
        carry_ref[...] = jnp.zeros_like(carry_ref)

    h = _rms_norm(x_ref[...], g_ref[...])
    h_ref[...] = h.astype(BF16)
    logits = jnp.dot(h, wr_ref[...], preferred_element_type=F32, precision=lax.Precision.HIGHEST)
    tc = logits.shape[0]
    lane = lax.broadcasted_iota(jnp.int32, logits.shape, 1)
    logits = jnp.where(lane < N_EXPERTS, logits, -jnp.inf)
    v1 = jnp.max(logits, axis=1, keepdims=True)
    i1 = jnp.min(jnp.where(logits == v1, lane, LANES), axis=1, keepdims=True)
    rest = jnp.where(lane == i1, -jnp.inf, logits)
    v2 = jnp.max(rest, axis=1, keepdims=True)
    i2 = jnp.min(jnp.where(rest == v2, lane, LANES), axis=1, keepdims=True)
    e2 = jnp.exp(v2 - v1)
    w1 = 1.0 / (1.0 + e2)
    w2 = e2 / (1.0 + e2)

    chosen = jnp.where((lane == i1) | (lane == i2), 1.0, 0.0)
    r_i = lax.broadcasted_iota(jnp.int32, (tc, tc), 0)
    c_i = lax.broadcasted_iota(jnp.int32, (tc, tc), 1)
    before = jnp.where(c_i < r_i, 1.0, 0.0).astype(BF16)
    carry = carry_ref[0:1, :]
    rank = _dot(before, chosen.astype(BF16)) + carry
    r1 = jnp.sum(jnp.where(lane == i1, rank, 0.0), axis=1, keepdims=True)
    r2 = jnp.sum(jnp.where(lane == i2, rank, 0.0), axis=1, keepdims=True)
    cols = (i1.astype(F32), i2.astype(F32), r1, r2, w1, w2)
    meta = jnp.zeros(logits.shape, F32)
    for idx, col in enumerate(cols):
        meta = jnp.where(lane == idx, col, meta)
    meta_ref[...] = meta
    metat_ref[...] = meta.T[0:8, :]
    cs_ref[0] = jnp.broadcast_to(carry, (8, LANES))
    total = carry + jnp.sum(chosen, axis=0, keepdims=True)
    carry_ref[...] = jnp.broadcast_to(total, (8, LANES))
    tot_ref[...] = jnp.broadcast_to(total, (8, LANES))


def _route(x, g, wr):
    n = x.shape[0]
    tc = MOE_TC
    n_c = n // tc
    return pl.pallas_call(
        _route_kernel,
        grid=(n_c,),
        in_specs=[
            pl.BlockSpec((tc, D_MODEL), lambda c: (c, 0)),
            pl.BlockSpec((1, D_MODEL), lambda c: (0, 0)),
            pl.BlockSpec((D_MODEL, LANES), lambda c: (0, 0)),
        ],
        out_specs=[
            pl.BlockSpec((tc, D_MODEL), lambda c: (c, 0)),
            pl.BlockSpec((tc, LANES), lambda c: (c, 0)),
            pl.BlockSpec((8, tc), lambda c: (0, c)),
            pl.BlockSpec((1, 8, LANES), lambda c: (c, 0, 0)),
            pl.BlockSpec((8, LANES), lambda c: (0, 0)),
        ],
        out_shape=[
            jax.ShapeDtypeStruct((n, D_MODEL), BF16),
            jax.ShapeDtypeStruct((n, LANES), F32),
            jax.ShapeDtypeStruct((8, n), F32),
            jax.ShapeDtypeStruct((n_c, 8, LANES), F32),
            jax.ShapeDtypeStruct((8, LANES), F32),
        ],
        scratch_shapes=[pltpu.VMEM((8, LANES), F32)],
        compiler_params=_params("arbitrary"),
        name="route",
    )(x, g, wr)


def _moe_plan(cs, tot, n):
    tm, tg, tc = MOE_TM, MOE_TG, MOE_TC
    n_c = n // tc
    n_tiles = 2 * n // tm + N_EXPERTS
    n_g = n_tiles * tm // tg
    k_max = n_g + N_EXPERTS * n_c
    i32 = jnp.int32
    counts = tot[0, :N_EXPERTS].astype(i32)
    cs = cs[:, 0, :N_EXPERTS].astype(i32)
    padded = (counts + tm - 1) // tm * tm
    ends = jnp.cumsum(padded)
    off = ends - padded
    tile_start = jnp.arange(n_tiles, dtype=i32) * tm
    tile_e = jnp.minimum(jnp.sum(tile_start[:, None] >= ends[None, :], axis=1), N_EXPERTS - 1).astype(i32)
    n_valid_tiles = (ends[-1] // tm).astype(i32).reshape(1)

    p0 = jnp.arange(n_g, dtype=i32) * tg
    g_e = tile_e[p0 // tm]
    lo = p0 - off[g_e]
    cs_g = cs[:, g_e].T
    c_lo = jnp.maximum(jnp.sum(cs_g <= lo[:, None], axis=1) - 1, 0)
    c_hi = jnp.maximum(jnp.sum(cs_g < (lo + tg)[:, None], axis=1) - 1, c_lo)
    n_it = c_hi - c_lo + 1
    it_end = jnp.cumsum(n_it)
    it_start = it_end - n_it
    n_items = it_end[-1].astype(i32).reshape(1)
    k = jnp.arange(k_max, dtype=i32)
    valid = k < n_items[0]
    item_g = jnp.minimum(jnp.sum(k[:, None] >= it_end[None, :], axis=1), n_g - 1).astype(i32)
    item_c = jnp.minimum(c_lo[item_g] + k - it_start[item_g], c_hi[item_g]).astype(i32)

    order = jnp.argsort(jnp.where(valid, item_c * n_g + item_g, n_c * n_g + k))
    last_valid = n_items[0] - 1
    cc = item_c[order]
    cg = item_g[order]
    cc = jnp.where(valid, cc, cc[last_valid]).astype(i32)
    cg = jnp.where(valid, cg, cg[last_valid]).astype(i32)
    prev_c = jnp.concatenate([jnp.full((1,), -1, i32), cc[:-1]])
    next_c = jnp.concatenate([cc[1:], jnp.full((1,), -1, i32)])
    comb_first = (valid & (cc != prev_c)).astype(i32)
    comb_last = (valid & ((cc != next_c) | (k == last_valid))).astype(i32)
    slots = MOE_SLOTS
    n_gi = (n_it + slots - 1) // slots
    gi_end = jnp.cumsum(n_gi)
    gi_start = gi_end - n_gi
    n_gitems = gi_end[-1].astype(i32).reshape(1)
    kg_max = n_g + (N_EXPERTS * n_c + slots - 1) // slots
    kg = jnp.arange(kg_max, dtype=i32)
    gat_g = jnp.minimum(jnp.sum(kg[:, None] >= gi_end[None, :], axis=1), n_g - 1).astype(i32)
    gat_hi = c_hi[gat_g].astype(i32)
    gat_c = jnp.minimum(c_lo[gat_g] + (kg - gi_start[gat_g]) * slots, gat_hi).astype(i32)
    gat_first = ((kg == gi_start[gat_g]) & (kg < n_gitems[0])).astype(i32)
    return dict(tile_e=tile_e, n_valid_tiles=n_valid_tiles, g_e=g_e.astype(i32), off=off.astype(i32),
                n_items=n_items, gat_g=gat_g, gat_c=gat_c, gat_hi=gat_hi, gat_first=gat_first,
                n_gitems=n_gitems, kg_max=kg_max,
                comb_c=cc, comb_g=cg, comb_first=comb_first, comb_last=comb_last,
                n_tiles=n_tiles, n_g=n_g, k_max=k_max)


def _gather_kernel(ig_ref, ic_ref, ihi_ref, first_ref, n_ref, ge_ref, off_ref, *refs):
    h_refs, mt_refs, o_ref = refs[:MOE_SLOTS], refs[MOE_SLOTS:2 * MOE_SLOTS], refs[2 * MOE_SLOTS]
    k = pl.program_id(0)
    tg, tc = o_ref.shape[0], h_refs[0].shape[0]

    @pl.when(k < n_ref[0])
    def _():
        g = ig_ref[k]
        e = ge_ref[g]
        base = (off_ref[e] - g * tg).astype(F32)
        ef = e.astype(F32)
        row = lax.broadcasted_iota(jnp.int32, (tg, tc), 0).astype(F32)

        @pl.when(first_ref[k] == 1)
        def _():
            o_ref[...] = jnp.zeros_like(o_ref)

        for s in range(MOE_SLOTS):
            @pl.when(ic_ref[k] + s <= ihi_ref[k])
            def _(s=s):
                mt = mt_refs[s][...]
                tgt1 = jnp.where(mt[META_E1:META_E1 + 1] == ef, mt[META_R1:META_R1 + 1] + base, -1.0)
                tgt2 = jnp.where(mt[META_E2:META_E2 + 1] == ef, mt[META_R2:META_R2 + 1] + base, -1.0)
                pick = jnp.where((row == tgt1) | (row == tgt2), 1.0, 0.0).astype(BF16)
                o_ref[...] += _dot(pick, h_refs[s][...]).astype(BF16)


def _gather(plan, h, metat):
    tg, tc = MOE_TG, MOE_TC

    def chunk(k, ic, ihi, s):
        return jnp.minimum(ic[k] + s, ihi[k])

    h_specs = [pl.BlockSpec((tc, D_MODEL), lambda k, ig, ic, ihi, *_, s=s: (chunk(k, ic, ihi, s), 0))
               for s in range(MOE_SLOTS)]
    mt_specs = [pl.BlockSpec((8, tc), lambda k, ig, ic, ihi, *_, s=s: (0, chunk(k, ic, ihi, s)))
                for s in range(MOE_SLOTS)]
    grid_spec = pltpu.PrefetchScalarGridSpec(
        num_scalar_prefetch=7,
        grid=(plan["kg_max"],),
        in_specs=h_specs + mt_specs,
        out_specs=pl.BlockSpec((tg, D_MODEL), lambda k, ig, *_: (ig[k], 0)),
    )
    return pl.pallas_call(
        _gather_kernel,
        grid_spec=grid_spec,
        out_shape=jax.ShapeDtypeStruct((plan["n_g"] * tg, D_MODEL), BF16),
        compiler_params=_params("arbitrary"),
        name="moe_gather",
    )(plan["gat_g"], plan["gat_c"], plan["gat_hi"], plan["gat_first"], plan["n_gitems"], plan["g_e"],
      plan["off"], *([h] * MOE_SLOTS), *([metat] * MOE_SLOTS))


def _ffn_kernel(te_ref, nv_ref, xs_ref, wg_ref, wu_ref, wd_ref, o_ref, acc_ref):
    i = pl.program_id(0)
    f = pl.program_id(1)
    valid = i < nv_ref[0]

    @pl.when(valid & (f == 0))
    def _():
        acc_ref[...] = jnp.zeros_like(acc_ref)

    @pl.when(valid)
    def _():
        xs = xs_ref[...]
        a = _dot(xs, wg_ref[0])
        b = _dot(xs, wu_ref[0])
        t = (a * _sigmoid(a) * b).astype(BF16)
        acc_ref[...] += _dot(t, wd_ref[0])

    last = f == pl.num_programs(1) - 1

    @pl.when(valid & last)
    def _():
        o_ref[...] = acc_ref[...].astype(o_ref.dtype)

    @pl.when(jnp.logical_not(valid) & last)
    def _():
        o_ref[...] = jnp.zeros_like(o_ref)


def _ffn(plan, xs, wg, wu, wd):
    tm, tf = MOE_TM, MOE_TF
    d_ff = wg.shape[2]
    n_f = d_ff // tf

    def f_eff(i, f, nv):
        return jnp.where(i < nv[0], f, n_f - 1)

    grid_spec = pltpu.PrefetchScalarGridSpec(
        num_scalar_prefetch=2,
        grid=(plan["n_tiles"], n_f),
        in_specs=[
            pl.BlockSpec((tm, D_MODEL), lambda i, f, te, nv: (i, 0)),
            pl.BlockSpec((1, D_MODEL, tf), lambda i, f, te, nv: (te[i], 0, f_eff(i, f, nv))),
            pl.BlockSpec((1, D_MODEL, tf), lambda i, f, te, nv: (te[i], 0, f_eff(i, f, nv))),
            pl.BlockSpec((1, tf, D_MODEL), lambda i, f, te, nv: (te[i], f_eff(i, f, nv), 0)),
        ],
        out_specs=pl.BlockSpec((tm, D_MODEL), lambda i, f, te, nv: (i, 0)),
        scratch_shapes=[pltpu.VMEM((tm, D_MODEL), F32)],
    )
    return pl.pallas_call(
        _ffn_kernel,
        grid_spec=grid_spec,
        out_shape=jax.ShapeDtypeStruct(xs.shape, BF16),
        compiler_params=_params("arbitrary", "arbitrary"),
        name="moe_ffn",
    )(plan["tile_e"], plan["n_valid_tiles"], xs, wg, wu, wd)


def _combine_kernel(cc_ref, cg_ref, first_ref, last_ref, n_ref, ge_ref, off_ref,
                    x_ref, meta_ref, y_ref, gf_ref, o_ref, acc_ref):
    k = pl.program_id(0)
    tg, tc = y_ref.shape[0], x_ref.shape[0]

    @pl.when(k < n_ref[0])
    def _():
        g = cg_ref[k]
        e = ge_ref[g]
        base = (off_ref[e] - g * tg).astype(F32)
        ef = e.astype(F32)
        meta = meta_ref[...]
        col = lambda c: meta[:, c:c + 1]
        tgt1 = jnp.where(col(META_E1) == ef, col(META_R1) + base, -1.0)
        tgt2 = jnp.where(col(META_E2) == ef, col(META_R2) + base, -1.0)
        pos = lax.broadcasted_iota(jnp.int32, (tc, tg), 1).astype(F32)
        spread = (jnp.where(pos == tgt1, col(META_W1), 0.0) + jnp.where(pos == tgt2, col(META_W2), 0.0))
        part = _dot(spread.astype(BF16), y_ref[...])

        @pl.when(first_ref[k] == 1)
        def _():
            acc_ref[...] = part

        @pl.when(first_ref[k] == 0)
        def _():
            acc_ref[...] += part

        @pl.when(last_ref[k] == 1)
        def _():
            o_ref[...] = _rms_norm(x_ref[...] + acc_ref[...], gf_ref[...])


def _combine(plan, x, meta, y, g_final):
    n = x.shape[0]
    tg, tc = MOE_TG, MOE_TC
    grid_spec = pltpu.PrefetchScalarGridSpec(
        num_scalar_prefetch=7,
        grid=(plan["k_max"],),
        in_specs=[
            pl.BlockSpec((tc, D_MODEL), lambda k, cc, cg, *_: (cc[k], 0)),
            pl.BlockSpec((tc, LANES), lambda k, cc, cg, *_: (cc[k], 0)),
            pl.BlockSpec((tg, D_MODEL), lambda k, cc, cg, *_: (cg[k], 0)),
            pl.BlockSpec((1, D_MODEL), lambda k, cc, cg, *_: (0, 0)),
        ],
        out_specs=pl.BlockSpec((tc, D_MODEL), lambda k, cc, cg, *_: (cc[k], 0)),
        scratch_shapes=[pltpu.VMEM((tc, D_MODEL), F32)],
    )
    return pl.pallas_call(
        _combine_kernel,
        grid_spec=grid_spec,
        out_shape=jax.ShapeDtypeStruct((n, D_MODEL), F32),
        compiler_params=_params("arbitrary"),
        name="moe_combine",
    )(plan["comb_c"], plan["comb_g"], plan["comb_first"], plan["comb_last"], plan["n_items"],
      plan["g_e"], plan["off"], x, meta, y, g_final)


def _moe(x, g, wr, wg, wu, wd, g_final):
    n = x.shape[0]
    h, meta, metat, cs, tot = _route(x, g, wr)
    plan = _moe_plan(cs, tot, n)
    xs = _gather(plan, h, metat)
    y = _ffn(plan, xs, wg, wu, wd)
    return _combine(plan, x, meta, y, g_final)


def _prep_w_in(w_in):
    sizes = [WIDTH_A] * 3 + [WIDTH_B] * 3 + [D_MODEL] * 2
    qa, ka, va, qb, kb, vb, ga, gb = jnp.split(w_in, np.cumsum(sizes)[:-1].tolist(), axis=1)
    scale = HEAD_DIM ** -0.5
    return jnp.concatenate([ga, gb, qa * scale, ka, va, qb * scale, kb, vb], axis=1).astype(BF16)


def _rope_tables(seq):
    half = HEAD_DIM // 2
    inv = ROPE_THETA ** (-jnp.arange(half, dtype=F32) / half)
    ang = jnp.arange(seq, dtype=F32)[:, None] * inv[None, :]
    cos, sin = jnp.cos(ang), jnp.sin(ang)
    n_heads = WIDTH_B // HEAD_DIM
    return (jnp.tile(jnp.concatenate([cos, cos], axis=1), (1, n_heads)),
            jnp.tile(jnp.concatenate([-sin, sin], axis=1), (1, n_heads)))


def kernel(x, norm_mix_0, w_in_0, w_proj_a_0, w_proj_b_0, w_out_0, norm_ffn_0, w_gate_0, w_up_0, w_down_0,
           norm_mix_1, w_in_1, w_proj_a_1, w_proj_b_1, w_out_1, norm_ffn_1, w_router_1, w_gate_e_1, w_up_e_1,
           w_down_e_1, norm_final):
    bsz, seq, _ = x.shape
    assert seq % MOBA_BLOCK == 0 and seq % (DIL_STEPS * max(DIL_DILATIONS)) == 0
    cos, sin = _rope_tables(seq)
    row = lambda v: v.reshape(1, D_MODEL).astype(F32)
    xf = x.reshape(bsz * seq, D_MODEL)

    def mixer(xf, nm, w_in, wa, wb, wo):
        proj = _in_proj(xf, row(nm), _prep_w_in(w_in), cos, sin, tm=512)
        oa = _moba(proj, bsz, seq)
        ob = _dilated(proj, bsz, seq)
        return _merge(xf, oa, ob, proj, wa.astype(BF16), wb.astype(BF16), wo.astype(BF16), tm=512)

    xf = mixer(xf, norm_mix_0, w_in_0, w_proj_a_0, w_proj_b_0, w_out_0)
    xf = _swiglu(xf, row(norm_ffn_0), w_gate_0.astype(BF16), w_up_0.astype(BF16), w_down_0.astype(BF16),
                 tm=512, chunk=1024)
    xf = mixer(xf, norm_mix_1, w_in_1, w_proj_a_1, w_proj_b_1, w_out_1)
    wr = jnp.pad(w_router_1.astype(F32), ((0, 0), (0, LANES - N_EXPERTS)))
    xf = _moe(xf, row(norm_ffn_1), wr, w_gate_e_1.astype(BF16), w_up_e_1.astype(BF16),
              w_down_e_1.astype(BF16), row(norm_final))
    return xf.reshape(bsz, seq, D_MODEL)
```

```python
import functools

import jax
import jax.numpy as jnp
import numpy as np
from jax import lax
from jax.experimental import pallas as pl
from jax.experimental.pallas import tpu as pltpu

D_MODEL = 1024
HEAD_DIM = 64
MOBA_HEADS = 8
MOBA_BLOCK = 256
MOBA_TOPK = 3
DIL_DILATIONS = (1, 4, 16)
DIL_STEPS = 128
DIL_HEADS_PER_GROUP = 4
ROPE_THETA = 10000.0
RMS_EPS = 1e-6
N_EXPERTS = 8

WIDTH_A = MOBA_HEADS * HEAD_DIM
WIDTH_BG = DIL_HEADS_PER_GROUP * HEAD_DIM
WIDTH_B = len(DIL_DILATIONS) * WIDTH_BG

LANES = 128
NEG = -1e30
VMEM_LIMIT = 56 * 1024 * 1024

COL_GA = 0
COL_GB = COL_GA + D_MODEL
COL_QA = COL_GB + D_MODEL
COL_KA = COL_QA + WIDTH_A
COL_VA = COL_KA + WIDTH_A
COL_QB = COL_VA + WIDTH_A
COL_KB = COL_QB + WIDTH_B
COL_VB = COL_KB + WIDTH_B
PROJ_COLS = COL_VB + WIDTH_B

F32 = jnp.float32
BF16 = jnp.bfloat16


def _params(*sem):
    return pltpu.CompilerParams(dimension_semantics=sem, vmem_limit_bytes=VMEM_LIMIT)


def _dot(a, b):
    return jnp.dot(a, b, preferred_element_type=F32)


def _dot_nt(a, b):
    return lax.dot_general(a, b, (((1,), (1,)), ((), ())), preferred_element_type=F32)


def _rms_norm(x, g):
    return x * lax.rsqrt(jnp.mean(x * x, axis=-1, keepdims=True) + RMS_EPS) * g


def _sigmoid(z):
    return 1.0 / (1.0 + jnp.exp(-z))


def _rope_apply(acc, cos, sin_signed):
    width = acc.shape[-1]
    lane = lax.broadcasted_iota(jnp.int32, acc.shape, 1)
    first_half = (lane % HEAD_DIM) < (HEAD_DIM // 2)
    rot = jnp.where(first_half,
                    pltpu.roll(acc, width - HEAD_DIM // 2, 1),
                    pltpu.roll(acc, HEAD_DIM // 2, 1))
    return acc * cos + rot * sin_signed


def _in_proj_kernel(x_ref, g_ref, w_ref, cos_ref, sin_ref, o_ref):
    h = _rms_norm(x_ref[...], g_ref[...]).astype(BF16)
    plain = ((COL_GA, D_MODEL), (COL_GB, D_MODEL), (COL_VA, WIDTH_A), (COL_VB, WIDTH_B))
    rotary = ((COL_QA, WIDTH_A), (COL_KA, WIDTH_A), (COL_QB, WIDTH_B), (COL_KB, WIDTH_B))
    for c0, w in plain:
        o_ref[:, c0:c0 + w] = _dot(h, w_ref[:, c0:c0 + w]).astype(BF16)
    for c0, w in rotary:
        acc = _dot(h, w_ref[:, c0:c0 + w])
        o_ref[:, c0:c0 + w] = _rope_apply(acc, cos_ref[:, :w], sin_ref[:, :w]).astype(BF16)


def _in_proj(x, g, w, cos, sin, tm):
    n = x.shape[0]
    seq = cos.shape[0]
    n_seq_tiles = seq // tm
    return pl.pallas_call(
        _in_proj_kernel,
        grid=(n // tm,),
        in_specs=[
            pl.BlockSpec((tm, D_MODEL), lambda i: (i, 0)),
            pl.BlockSpec((1, D_MODEL), lambda i: (0, 0)),
            pl.BlockSpec((D_MODEL, PROJ_COLS), lambda i: (0, 0), pipeline_mode=pl.Buffered(1)),
            pl.BlockSpec((tm, WIDTH_B), lambda i: (i % n_seq_tiles, 0)),
            pl.BlockSpec((tm, WIDTH_B), lambda i: (i % n_seq_tiles, 0)),
        ],
        out_specs=pl.BlockSpec((tm, PROJ_COLS), lambda i: (i, 0)),
        out_shape=jax.ShapeDtypeStruct((n, PROJ_COLS), BF16),
        compiler_params=_params("parallel"),
        name="in_proj",
    )(x, g, w, cos, sin)


def _moba_kernel(q_ref, k_ref, v_ref, o_ref, qa_ref, qb_ref, ka_ref, kb_ref, va_ref, vb_ref, *, seq):
    n_blk = seq // MOBA_BLOCK
    q2 = q_ref[...]
    k2 = k_ref[...]
    lane = lax.broadcasted_iota(jnp.int32, (seq, LANES), 1)
    row = lax.broadcasted_iota(jnp.int32, (seq, LANES), 0)
    is_a = lane < HEAD_DIM

    kmean = jnp.mean(k2.astype(F32).reshape(n_blk, MOBA_BLOCK, LANES), axis=1)
    lane8 = lax.broadcasted_iota(jnp.int32, (n_blk, LANES), 1)
    km = jnp.concatenate([jnp.where(lane8 < HEAD_DIM, kmean, 0.0),
                          jnp.where(lane8 >= HEAD_DIM, kmean, 0.0)], axis=0)
    km_hi = km.astype(BF16)
    km_lo = (km - km_hi.astype(F32)).astype(BF16)
    g_all = _dot_nt(jnp.concatenate([km_hi, km_lo], axis=0), q2)
    gates = (g_all[0:n_blk] + g_all[2 * n_blk:3 * n_blk],
             g_all[n_blk:2 * n_blk] + g_all[3 * n_blk:4 * n_blk])

    blk = lax.broadcasted_iota(jnp.int32, (n_blk, seq), 0)
    q_blk = lax.broadcasted_iota(jnp.int32, (n_blk, seq), 1) // MOBA_BLOCK
    past = blk < q_blk
    bias_t = []
    for g in gates:
        g = jnp.where(past, g, -jnp.inf)
        rank = jnp.zeros((n_blk, seq), jnp.int32)
        for jp in range(n_blk):
            gj = g[jp:jp + 1, :]
            beats = (gj > g) | ((gj == g) & (jp < blk))
            rank = rank + beats.astype(jnp.int32)
        keep = ((rank < MOBA_TOPK) & past) | (blk == q_blk)
        bias_t.append(jnp.where(keep, 0.0, NEG))
    pad = jnp.zeros((HEAD_DIM - n_blk, seq), F32)
    bias = jnp.concatenate([bias_t[1], pad, bias_t[0], pad], axis=0).T.astype(BF16)
    key_blk = row // MOBA_BLOCK
    onehot = ((lane == key_blk) | (lane == key_blk + HEAD_DIM)).astype(BF16)

    v2 = v_ref[...]
    ones = jnp.ones_like(v2)
    qa_ref[...] = jnp.where(is_a, q2, bias)
    qb_ref[...] = jnp.where(is_a, bias, q2)
    ka_ref[...] = jnp.where(is_a, k2, onehot)
    kb_ref[...] = jnp.where(is_a, onehot, k2)
    va_ref[...] = jnp.where(is_a, v2, ones)
    vb_ref[...] = jnp.where(is_a, ones, v2)

    tri_r = lax.broadcasted_iota(jnp.int32, (MOBA_BLOCK, MOBA_BLOCK), 0)
    tri_c = lax.broadcasted_iota(jnp.int32, (MOBA_BLOCK, MOBA_BLOCK), 1)
    causal = tri_c <= tri_r
    out_lane = lax.broadcasted_iota(jnp.int32, (MOBA_BLOCK, LANES), 1) < HEAD_DIM

    def rows(j):
        return slice(j * MOBA_BLOCK, (j + 1) * MOBA_BLOCK)

    def half_max(s):
        return jnp.maximum(s[:, :LANES], s[:, LANES:])

    def head(qx_ref, kx_ref, vx_ref, i):
        qx = qx_ref[rows(i), :]
        s_own = jnp.where(causal, _dot_nt(qx, kx_ref[rows(i), :]), NEG)
        m_run = half_max(s_own)
        for j in range(i):
            m_run = jnp.maximum(m_run, half_max(_dot_nt(qx, kx_ref[rows(j), :])))
        m = jnp.max(m_run, axis=1, keepdims=True)
        acc = _dot(jnp.exp(s_own - m).astype(BF16), vx_ref[rows(i), :])
        for j in range(i):
            p = jnp.exp(_dot_nt(qx, kx_ref[rows(j), :]) - m)
            acc = acc + _dot(p.astype(BF16), vx_ref[rows(j), :])
        return acc * (1.0 / pltpu.roll(acc, HEAD_DIM, 1))

    for i in range(n_blk):
        out = jnp.where(out_lane, head(qa_ref, ka_ref, va_ref, i), head(qb_ref, kb_ref, vb_ref, i))
        o_ref[rows(i), :] = out.astype(o_ref.dtype)


def _moba(proj, bsz, seq):
    n = bsz * seq
    qc, kc, vc = COL_QA // LANES, COL_KA // LANES, COL_VA // LANES
    blk = (seq, LANES)
    return pl.pallas_call(
        functools.partial(_moba_kernel, seq=seq),
        grid=(bsz, WIDTH_A // LANES),
        in_specs=[
            pl.BlockSpec(blk, lambda b, h: (b, qc + h)),
            pl.BlockSpec(blk, lambda b, h: (b, kc + h)),
            pl.BlockSpec(blk, lambda b, h: (b, vc + h)),
        ],
        out_specs=pl.BlockSpec(blk, lambda b, h: (b, h)),
        out_shape=jax.ShapeDtypeStruct((n, WIDTH_A), BF16),
        scratch_shapes=[pltpu.VMEM(blk, BF16) for _ in range(6)],
        compiler_params=_params("parallel", "parallel"),
        name="moba",
    )(proj, proj, proj)


def _dilated_kernel(*refs, seq):
    n_groups = len(DIL_DILATIONS)
    in_refs = refs[:3 * n_groups]
    o_ref = refs[3 * n_groups]
    slab_ref, qres_ref, kres_ref, vres_ref, on_ref, ln_ref = refs[3 * n_groups + 1:]
    steps = DIL_STEPS
    n_pairs = WIDTH_BG // LANES
    n_row_blocks = seq // steps

    zeros_pad = jnp.zeros((steps, WIDTH_BG), BF16)
    kres_ref[0:steps, :] = zeros_pad
    vres_ref[0:steps, :] = zeros_pad

    qi = lax.broadcasted_iota(jnp.int32, (steps, 2 * steps), 0)
    kj = lax.broadcasted_iota(jnp.int32, (steps, 2 * steps), 1)
    band = (kj >= qi) & (kj <= qi + steps)
    cur_half = kj >= steps
    is_a = lax.broadcasted_iota(jnp.int32, (steps, LANES), 1) < HEAD_DIM

    for g, d in enumerate(DIL_DILATIONS):
        sub = seq // d
        n_blk = sub // steps
        q_ref, k_ref, v_ref = in_refs[3 * g:3 * g + 3]
        for src, dst, off in ((q_ref, qres_ref, 0), (k_ref, kres_ref, steps), (v_ref, vres_ref, steps)):
            if d == 1:
                dst[off:off + seq, :] = src[...]
                continue
            for hp in range(n_pairs):
                cols = slice(hp * LANES, (hp + 1) * LANES)
                slab_ref[...] = src[:, cols].astype(F32)
                for r in range(d):
                    dst[off + r * sub:off + (r + 1) * sub, cols] = (
                        slab_ref[pl.ds(r, sub, stride=d), :].astype(BF16))

        def row_block(c, _, d=d, n_blk=n_blk, g=g):
            rows_q = pl.ds(pl.multiple_of(c * steps, steps), steps)
            rows_kv = pl.ds(pl.multiple_of(c * steps, steps), 2 * steps)
            prev_valid = (c % n_blk) != 0
            mask = band & (cur_half | prev_valid)
            start = (c % n_blk) * (steps * d) + c // n_blk
            for hp in range(n_pairs):
                cols = slice(hp * LANES, (hp + 1) * LANES)
                q2 = qres_ref[rows_q, cols]
                k2 = kres_ref[rows_kv, cols]
                v2 = vres_ref[rows_kv, cols]
                outs, lses = [], []
                for head_a in (True, False):
                    qm = jnp.where(is_a if head_a else ~is_a, q2, jnp.zeros_like(q2))
                    s = jnp.where(mask, _dot_nt(qm, k2), NEG)
                    m = jnp.max(s, axis=1, keepdims=True)
                    p = jnp.exp(s - m)
                    l = jnp.sum(p, axis=1, keepdims=True)
                    outs.append(_dot(p.astype(BF16), v2) * (1.0 / l))
                    lses.append(jnp.broadcast_to(m + jnp.log(l), (steps, LANES)))
                o2 = jnp.where(is_a, outs[0], outs[1])
                l2 = jnp.where(is_a, lses[0], lses[1])
                slot = g * n_pairs + hp
                if d == 1:
                    on_ref[slot, rows_q, :] = o2
                    ln_ref[slot, rows_q, :] = l2
                else:
                    on_ref[slot, pl.ds(start, steps, stride=d), :] = o2
                    ln_ref[slot, pl.ds(start, steps, stride=d), :] = l2
            return 0

        lax.fori_loop(0, n_row_blocks, row_block, 0, unroll=4)

    for hp in range(n_pairs):
        ls = [ln_ref[g * n_pairs + hp] for g in range(n_groups)]
        mx = functools.reduce(jnp.maximum, ls)
        es = [jnp.exp(l - mx) for l in ls]
        inv = 1.0 / functools.reduce(lambda a, b: a + b, es)
        out = functools.reduce(lambda a, b: a + b,
                               [(es[g] * inv) * on_ref[g * n_pairs + hp] for g in range(n_groups)])
        o_ref[:, hp * LANES:(hp + 1) * LANES] = out.astype(o_ref.dtype)


def _dilated(proj, bsz, seq):
    n = bsz * seq
    n_groups = len(DIL_DILATIONS)
    blk = (seq, WIDTH_BG)
    in_specs = []
    for g in range(n_groups):
        for col in (COL_QB, COL_KB, COL_VB):
            cb = col // WIDTH_BG + g
            in_specs.append(pl.BlockSpec(blk, lambda b, cb=cb: (b, cb)))
    n_slots = n_groups * (WIDTH_BG // LANES)
    return pl.pallas_call(
        functools.partial(_dilated_kernel, seq=seq),
        grid=(bsz,),
        in_specs=in_specs,
        out_specs=pl.BlockSpec(blk, lambda b: (b, 0)),
        out_shape=jax.ShapeDtypeStruct((n, WIDTH_BG), BF16),
        scratch_shapes=[
            pltpu.VMEM((seq, LANES), F32),
            pltpu.VMEM((seq, WIDTH_BG), BF16),
            pltpu.VMEM((seq + DIL_STEPS, WIDTH_BG), BF16),
            pltpu.VMEM((seq + DIL_STEPS, WIDTH_BG), BF16),
            pltpu.VMEM((n_slots, seq, LANES), F32),
            pltpu.VMEM((n_slots, seq, LANES), F32),
        ],
        compiler_params=_params("parallel"),
        name="dilated",
    )(*([proj] * (3 * n_groups)))


def _merge_kernel(x_ref, oa_ref, ob_ref, ga_ref, gb_ref, wa_ref, wb_ref, wo_ref, o_ref):
    ya = _dot(oa_ref[...], wa_ref[...])
    yb = _dot(ob_ref[...], wb_ref[...])
    merged = _sigmoid(ga_ref[...].astype(F32)) * ya + _sigmoid(gb_ref[...].astype(F32)) * yb
    o_ref[...] = x_ref[...] + _dot(merged.astype(BF16), wo_ref[...])


def _merge(x, oa, ob, proj, wa, wb, wo, tm):
    n = x.shape[0]
    const = lambda i: (0, 0)
    return pl.pallas_call(
        _merge_kernel,
        grid=(n // tm,),
        in_specs=[
            pl.BlockSpec((tm, D_MODEL), lambda i: (i, 0)),
            pl.BlockSpec((tm, WIDTH_A), lambda i: (i, 0)),
            pl.BlockSpec((tm, WIDTH_BG), lambda i: (i, 0)),
            pl.BlockSpec((tm, D_MODEL), lambda i: (i, COL_GA // D_MODEL)),
            pl.BlockSpec((tm, D_MODEL), lambda i: (i, COL_GB // D_MODEL)),
            pl.BlockSpec((WIDTH_A, D_MODEL), const),
            pl.BlockSpec((WIDTH_BG, D_MODEL), const),
            pl.BlockSpec((D_MODEL, D_MODEL), const),
        ],
        out_specs=pl.BlockSpec((tm, D_MODEL), lambda i: (i, 0)),
        out_shape=jax.ShapeDtypeStruct((n, D_MODEL), F32),
        compiler_params=_params("parallel"),
        name="merge",
    )(x, oa, ob, proj, proj, wa, wb, wo)


def _swiglu_kernel(x_ref, g_ref, wg_ref, wu_ref, wd_ref, o_ref, *, chunk):
    x = x_ref[...]
    h = _rms_norm(x, g_ref[...]).astype(BF16)
    d_ff = wg_ref.shape[1]
    acc = x
    for c0 in range(0, d_ff, chunk):
        cols = slice(c0, min(c0 + chunk, d_ff))
        a = _dot(h, wg_ref[:, cols])
        b = _dot(h, wu_ref[:, cols])
        t = (a * _sigmoid(a) * b).astype(BF16)
        acc = acc + _dot(t, wd_ref[cols, :])
    o_ref[...] = acc


def _swiglu(x, g, wg, wu, wd, tm, chunk):
    n = x.shape[0]
    d_ff = wg.shape[1]
    const = lambda i: (0, 0)
    return pl.pallas_call(
        functools.partial(_swiglu_kernel, chunk=chunk),
        grid=(n // tm,),
        in_specs=[
            pl.BlockSpec((tm, D_MODEL), lambda i: (i, 0)),
            pl.BlockSpec((1, D_MODEL), const),
            pl.BlockSpec((D_MODEL, d_ff), const, pipeline_mode=pl.Buffered(1)),
            pl.BlockSpec((D_MODEL, d_ff), const, pipeline_mode=pl.Buffered(1)),
            pl.BlockSpec((d_ff, D_MODEL), const, pipeline_mode=pl.Buffered(1)),
        ],
        out_specs=pl.BlockSpec((tm, D_MODEL), lambda i: (i, 0)),
        out_shape=jax.ShapeDtypeStruct((n, D_MODEL), F32),
        compiler_params=_params("parallel"),
        name="swiglu",
    )(x, g, wg, wu, wd)


MOE_TM = 1024
MOE_TG = 256
MOE_TC = 512
MOE_SLOTS = 3
MOE_CSLOTS = 4
MOE_TF = 512
META_E1, META_E2, META_R1, META_R2, META_W1, META_W2 = range(6)


def _route_kernel(x_ref, g_ref, wr_ref, h_ref, meta_ref, metat_ref, cs_ref, tot_ref, carry_ref):
    @pl.when(pl.program_id(0) == 0)
    def _():
        carry_ref[...] = jnp.zeros_like(carry_ref)

    h = _rms_norm(x_ref[...], g_ref[...])
    h_ref[...] = h.astype(BF16)
    logits = jnp.dot(h, wr_ref[...], preferred_element_type=F32, precision=lax.Precision.HIGHEST)
    tc = logits.shape[0]
    lane = lax.broadcasted_iota(jnp.int32, logits.shape, 1)
    logits = jnp.where(lane < N_EXPERTS, logits, -jnp.inf)
    v1 = jnp.max(logits, axis=1, keepdims=True)
    i1 = jnp.min(jnp.where(logits == v1, lane, LANES), axis=1, keepdims=True)
    rest = jnp.where(lane == i1, -jnp.inf, logits)
    v2 = jnp.max(rest, axis=1, keepdims=True)
    i2 = jnp.min(jnp.where(rest == v2, lane, LANES), axis=1, keepdims=True)
    e2 = jnp.exp(v2 - v1)
    w1 = 1.0 / (1.0 + e2)
    w2 = e2 / (1.0 + e2)

    chosen = jnp.where((lane == i1) | (lane == i2), 1.0, 0.0)
    r_i = lax.broadcasted_iota(jnp.int32, (tc, tc), 0)
    c_i = lax.broadcasted_iota(jnp.int32, (tc, tc), 1)
    before = jnp.where(c_i < r_i, 1.0, 0.0).astype(BF16)
    carry = carry_ref[0:1, :]
    rank = _dot(before, chosen.astype(BF16)) + carry
    r1 = jnp.sum(jnp.where(lane == i1, rank, 0.0), axis=1, keepdims=True)
    r2 = jnp.sum(jnp.where(lane == i2, rank, 0.0), axis=1, keepdims=True)
    cols = (i1.astype(F32), i2.astype(F32), r1, r2, w1, w2)
    meta = jnp.zeros(logits.shape, F32)
    for idx, col in enumerate(cols):
        meta = jnp.where(lane == idx, col, meta)
    meta_ref[...] = meta
    metat_ref[...] = meta.T[0:8, :]
    cs_ref[0] = jnp.broadcast_to(carry, (8, LANES))
    total = carry + jnp.sum(chosen, axis=0, keepdims=True)
    carry_ref[...] = jnp.broadcast_to(total, (8, LANES))
    tot_ref[...] = jnp.broadcast_to(total, (8, LANES))


def _route(x, g, wr):
    n = x.shape[0]
    tc = MOE_TC
    n_c = n // tc
    return pl.pallas_call(
        _route_kernel,
        grid=(n_c,),
        in_specs=[
            pl.BlockSpec((tc, D_MODEL), lambda c: (c, 0)),
            pl.BlockSpec((1, D_MODEL), lambda c: (0, 0)),
            pl.BlockSpec((D_MODEL, LANES), lambda c: (0, 0)),
        ],
        out_specs=[
            pl.BlockSpec((tc, D_MODEL), lambda c: (c, 0)),
            pl.BlockSpec((tc, LANES), lambda c: (c, 0)),
            pl.BlockSpec((8, tc), lambda c: (0, c)),
            pl.BlockSpec((1, 8, LANES), lambda c: (c, 0, 0)),
            pl.BlockSpec((8, LANES), lambda c: (0, 0)),
        ],
        out_shape=[
            jax.ShapeDtypeStruct((n, D_MODEL), BF16),
            jax.ShapeDtypeStruct((n, LANES), F32),
            jax.ShapeDtypeStruct((8, n), F32),
            jax.ShapeDtypeStruct((n_c, 8, LANES), F32),
            jax.ShapeDtypeStruct((8, LANES), F32),
        ],
        scratch_shapes=[pltpu.VMEM((8, LANES), F32)],
        compiler_params=_params("arbitrary"),
        name="route",
    )(x, g, wr)


def _moe_plan(cs, tot, n):
    tm, tg, tc = MOE_TM, MOE_TG, MOE_TC
    n_c = n // tc
    n_tiles = 2 * n // tm + N_EXPERTS
    n_g = n_tiles * tm // tg
    k_max = n_g + N_EXPERTS * n_c
    i32 = jnp.int32
    counts = tot[0, :N_EXPERTS].astype(i32)
    cs = cs[:, 0, :N_EXPERTS].astype(i32)
    padded = (counts + tm - 1) // tm * tm
    ends = jnp.cumsum(padded)
    off = ends - padded
    tile_start = jnp.arange(n_tiles, dtype=i32) * tm
    tile_e = jnp.minimum(jnp.sum(tile_start[:, None] >= ends[None, :], axis=1), N_EXPERTS - 1).astype(i32)
    n_valid_tiles = (ends[-1] // tm).astype(i32).reshape(1)

    p0 = jnp.arange(n_g, dtype=i32) * tg
    g_e = tile_e[p0 // tm]
    lo = p0 - off[g_e]
    cs_g = cs[:, g_e].T
    c_lo = jnp.maximum(jnp.sum(cs_g <= lo[:, None], axis=1) - 1, 0)
    c_hi = jnp.maximum(jnp.sum(cs_g < (lo + tg)[:, None], axis=1) - 1, c_lo)
    n_it = c_hi - c_lo + 1
    it_end = jnp.cumsum(n_it)
    it_start = it_end - n_it
    n_items = it_end[-1].astype(i32).reshape(1)
    k = jnp.arange(k_max, dtype=i32)
    valid = k < n_items[0]
    item_g = jnp.minimum(jnp.sum(k[:, None] >= it_end[None, :], axis=1), n_g - 1).astype(i32)
    item_c = jnp.minimum(c_lo[item_g] + k - it_start[item_g], c_hi[item_g]).astype(i32)

    order = jnp.argsort(jnp.where(valid, item_c * n_g + item_g, n_c * n_g + k))
    last_valid = n_items[0] - 1
    cc = item_c[order]
    cg = item_g[order]
    cc = jnp.where(valid, cc, n_c).astype(i32)
    cs_n = MOE_CSLOTS
    chunk_ids = jnp.arange(n_c + 1, dtype=i32)
    c_first = jnp.sum(cc[None, :] < chunk_ids[:, None], axis=1).astype(i32)
    m_c = c_first[1:] - c_first[:-1]
    n_ci = (m_c + cs_n - 1) // cs_n
    ci_end = jnp.cumsum(n_ci)
    ci_start = ci_end - n_ci
    n_citems = ci_end[-1].astype(i32).reshape(1)
    kc_max = n_c + (k_max + cs_n - 1) // cs_n
    kc = jnp.arange(kc_max, dtype=i32)
    comb_c = jnp.minimum(jnp.sum(kc[:, None] >= ci_end[None, :], axis=1), n_c - 1).astype(i32)
    run = jnp.minimum(kc - ci_start[comb_c], n_ci[comb_c] - 1)
    comb_first = ((run == 0) & (kc < n_citems[0])).astype(i32)
    comb_last = ((run == n_ci[comb_c] - 1) & (kc < n_citems[0])).astype(i32)
    comb_n = jnp.clip(m_c[comb_c] - run * cs_n, 0, cs_n).astype(i32)
    slot_idx = (c_first[comb_c] + run * cs_n)[None, :] + jnp.arange(cs_n, dtype=i32)[:, None]
    slot_idx = jnp.minimum(slot_idx, (c_first[comb_c] + m_c[comb_c] - 1)[None, :])
    comb_g = cg[slot_idx].astype(i32).reshape(-1)
    slots = MOE_SLOTS
    n_gi = (n_it + slots - 1) // slots
    gi_end = jnp.cumsum(n_gi)
    gi_start = gi_end - n_gi
    n_gitems = gi_end[-1].astype(i32).reshape(1)
    kg_max = n_g + (N_EXPERTS * n_c + slots - 1) // slots
    kg = jnp.arange(kg_max, dtype=i32)
    gat_g = jnp.minimum(jnp.sum(kg[:, None] >= gi_end[None, :], axis=1), n_g - 1).astype(i32)
    gat_hi = c_hi[gat_g].astype(i32)
    gat_c = jnp.minimum(c_lo[gat_g] + (kg - gi_start[gat_g]) * slots, gat_hi).astype(i32)
    gat_first = ((kg == gi_start[gat_g]) & (kg < n_gitems[0])).astype(i32)
    return dict(tile_e=tile_e, n_valid_tiles=n_valid_tiles, g_e=g_e.astype(i32), off=off.astype(i32),
                n_items=n_items, gat_g=gat_g, gat_c=gat_c, gat_hi=gat_hi, gat_first=gat_first,
                n_gitems=n_gitems, kg_max=kg_max,
                comb_c=comb_c, comb_g=comb_g, comb_n=comb_n, comb_first=comb_first, comb_last=comb_last,
                n_citems=n_citems, kc_max=kc_max, n_tiles=n_tiles, n_g=n_g)


def _gather_kernel(ig_ref, ic_ref, ihi_ref, first_ref, n_ref, ge_ref, off_ref, *refs):
    h_refs, mt_refs, o_ref = refs[:MOE_SLOTS], refs[MOE_SLOTS:2 * MOE_SLOTS], refs[2 * MOE_SLOTS]
    k = pl.program_id(0)
    tg, tc = o_ref.shape[0], h_refs[0].shape[0]

    @pl.when(k < n_ref[0])
    def _():
        g = ig_ref[k]
        e = ge_ref[g]
        base = (off_ref[e] - g * tg).astype(F32)
        ef = e.astype(F32)
        row = lax.broadcasted_iota(jnp.int32, (tg, tc), 0).astype(F32)

        @pl.when(first_ref[k] == 1)
        def _():
            o_ref[...] = jnp.zeros_like(o_ref)

        for s in range(MOE_SLOTS):
            @pl.when(ic_ref[k] + s <= ihi_ref[k])
            def _(s=s):
                mt = mt_refs[s][...]
                tgt1 = jnp.where(mt[META_E1:META_E1 + 1] == ef, mt[META_R1:META_R1 + 1] + base, -1.0)
                tgt2 = jnp.where(mt[META_E2:META_E2 + 1] == ef, mt[META_R2:META_R2 + 1] + base, -1.0)
                pick = jnp.where((row == tgt1) | (row == tgt2), 1.0, 0.0).astype(BF16)
                o_ref[...] += _dot(pick, h_refs[s][...]).astype(BF16)


def _gather(plan, h, metat):
    tg, tc = MOE_TG, MOE_TC

    def chunk(k, ic, ihi, s):
        return jnp.minimum(ic[k] + s, ihi[k])

    h_specs = [pl.BlockSpec((tc, D_MODEL), lambda k, ig, ic, ihi, *_, s=s: (chunk(k, ic, ihi, s), 0))
               for s in range(MOE_SLOTS)]
    mt_specs = [pl.BlockSpec((8, tc), lambda k, ig, ic, ihi, *_, s=s: (0, chunk(k, ic, ihi, s)))
                for s in range(MOE_SLOTS)]
    grid_spec = pltpu.PrefetchScalarGridSpec(
        num_scalar_prefetch=7,
        grid=(plan["kg_max"],),
        in_specs=h_specs + mt_specs,
        out_specs=pl.BlockSpec((tg, D_MODEL), lambda k, ig, *_: (ig[k], 0)),
    )
    return pl.pallas_call(
        _gather_kernel,
        grid_spec=grid_spec,
        out_shape=jax.ShapeDtypeStruct((plan["n_g"] * tg, D_MODEL), BF16),
        compiler_params=_params("arbitrary"),
        name="moe_gather",
    )(plan["gat_g"], plan["gat_c"], plan["gat_hi"], plan["gat_first"], plan["n_gitems"], plan["g_e"],
      plan["off"], *([h] * MOE_SLOTS), *([metat] * MOE_SLOTS))


def _ffn_kernel(te_ref, nv_ref, xs_ref, wg_ref, wu_ref, wd_ref, o_ref, acc_ref):
    i = pl.program_id(0)
    f = pl.program_id(1)
    valid = i < nv_ref[0]

    @pl.when(valid & (f == 0))
    def _():
        acc_ref[...] = jnp.zeros_like(acc_ref)

    @pl.when(valid)
    def _():
        xs = xs_ref[...]
        a = _dot(xs, wg_ref[0])
        b = _dot(xs, wu_ref[0])
        t = (a * _sigmoid(a) * b).astype(BF16)
        acc_ref[...] += _dot(t, wd_ref[0])

    last = f == pl.num_programs(1) - 1

    @pl.when(valid & last)
    def _():
        o_ref[...] = acc_ref[...].astype(o_ref.dtype)

    @pl.when(jnp.logical_not(valid) & last)
    def _():
        o_ref[...] = jnp.zeros_like(o_ref)


def _ffn(plan, xs, wg, wu, wd):
    tm, tf = MOE_TM, MOE_TF
    d_ff = wg.shape[2]
    n_f = d_ff // tf

    def f_eff(i, f, nv):
        return jnp.where(i < nv[0], f, n_f - 1)

    grid_spec = pltpu.PrefetchScalarGridSpec(
        num_scalar_prefetch=2,
        grid=(plan["n_tiles"], n_f),
        in_specs=[
            pl.BlockSpec((tm, D_MODEL), lambda i, f, te, nv: (i, 0)),
            pl.BlockSpec((1, D_MODEL, tf), lambda i, f, te, nv: (te[i], 0, f_eff(i, f, nv))),
            pl.BlockSpec((1, D_MODEL, tf), lambda i, f, te, nv: (te[i], 0, f_eff(i, f, nv))),
            pl.BlockSpec((1, tf, D_MODEL), lambda i, f, te, nv: (te[i], f_eff(i, f, nv), 0)),
        ],
        out_specs=pl.BlockSpec((tm, D_MODEL), lambda i, f, te, nv: (i, 0)),
        scratch_shapes=[pltpu.VMEM((tm, D_MODEL), F32)],
    )
    return pl.pallas_call(
        _ffn_kernel,
        grid_spec=grid_spec,
        out_shape=jax.ShapeDtypeStruct(xs.shape, BF16),
        compiler_params=_params("arbitrary", "arbitrary"),
        name="moe_ffn",
    )(plan["tile_e"], plan["n_valid_tiles"], xs, wg, wu, wd)


def _combine_kernel(cc_ref, cg_ref, cn_ref, first_ref, last_ref, n_ref, ge_ref, off_ref,
                    x_ref, meta_ref, *refs):
    y_refs = refs[:MOE_CSLOTS]
    gf_ref, o_ref, acc_ref = refs[MOE_CSLOTS:]
    k = pl.program_id(0)
    kc_max = pl.num_programs(0)
    tg, tc = y_refs[0].shape[0], x_ref.shape[0]

    @pl.when(k < n_ref[0])
    def _():
        meta = meta_ref[...]
        col = lambda c: meta[:, c:c + 1]
        pos = lax.broadcasted_iota(jnp.int32, (tc, tg), 1).astype(F32)
        part = None
        for s in range(MOE_CSLOTS):
            g = cg_ref[s * kc_max + k]
            e = ge_ref[g]
            ef = jnp.where(s < cn_ref[k], e, -1).astype(F32)
            base = (off_ref[e] - g * tg).astype(F32)
            tgt1 = jnp.where(col(META_E1) == ef, col(META_R1) + base, -1.0)
            tgt2 = jnp.where(col(META_E2) == ef, col(META_R2) + base, -1.0)
            spread = (jnp.where(pos == tgt1, col(META_W1), 0.0)
                      + jnp.where(pos == tgt2, col(META_W2), 0.0))
            term = _dot(spread.astype(BF16), y_refs[s][...])
            part = term if part is None else part + term

        @pl.when(first_ref[k] == 1)
        def _():
            acc_ref[...] = part

        @pl.when(first_ref[k] == 0)
        def _():
            acc_ref[...] += part

        @pl.when(last_ref[k] == 1)
        def _():
            o_ref[...] = _rms_norm(x_ref[...] + acc_ref[...], gf_ref[...])


def _combine(plan, x, meta, y, g_final):
    n = x.shape[0]
    tg, tc = MOE_TG, MOE_TC
    kc_max = plan["kc_max"]
    y_specs = [pl.BlockSpec((tg, D_MODEL), lambda k, cc, cg, *_, s=s: (cg[s * kc_max + k], 0))
               for s in range(MOE_CSLOTS)]
    grid_spec = pltpu.PrefetchScalarGridSpec(
        num_scalar_prefetch=8,
        grid=(kc_max,),
        in_specs=[
            pl.BlockSpec((tc, D_MODEL), lambda k, cc, *_: (cc[k], 0)),
            pl.BlockSpec((tc, LANES), lambda k, cc, *_: (cc[k], 0)),
            *y_specs,
            pl.BlockSpec((1, D_MODEL), lambda k, cc, *_: (0, 0)),
        ],
        out_specs=pl.BlockSpec((tc, D_MODEL), lambda k, cc, *_: (cc[k], 0)),
        scratch_shapes=[pltpu.VMEM((tc, D_MODEL), F32)],
    )
    return pl.pallas_call(
        _combine_kernel,
        grid_spec=grid_spec,
        out_shape=jax.ShapeDtypeStruct((n, D_MODEL), F32),
        compiler_params=_params("arbitrary"),
        name="moe_combine",
    )(plan["comb_c"], plan["comb_g"], plan["comb_n"], plan["comb_first"], plan["comb_last"],
      plan["n_citems"], plan["g_e"], plan["off"], x, meta, *([y] * MOE_CSLOTS), g_final)


def _moe(x, g, wr, wg, wu, wd, g_final):
    n = x.shape[0]
    h, meta, metat, cs, tot = _route(x, g, wr)
    plan = _moe_plan(cs, tot, n)
    xs = _gather(plan, h, metat)
    y = _ffn(plan, xs, wg, wu, wd)
    return _combine(plan, x, meta, y, g_final)


def _prep_w_in(w_in):
    sizes = [WIDTH_A] * 3 + [WIDTH_B] * 3 + [D_MODEL] * 2
    qa, ka, va, qb, kb, vb, ga, gb = jnp.split(w_in, np.cumsum(sizes)[:-1].tolist(), axis=1)
    scale = HEAD_DIM ** -0.5
    return jnp.concatenate([ga, gb, qa * scale, ka, va, qb * scale, kb, vb], axis=1).astype(BF16)


def _rope_tables(seq):
    half = HEAD_DIM // 2
    inv = ROPE_THETA ** (-jnp.arange(half, dtype=F32) / half)
    ang = jnp.arange(seq, dtype=F32)[:, None] * inv[None, :]
    cos, sin = jnp.cos(ang), jnp.sin(ang)
    n_heads = WIDTH_B // HEAD_DIM
    return (jnp.tile(jnp.concatenate([cos, cos], axis=1), (1, n_heads)),
            jnp.tile(jnp.concatenate([-sin, sin], axis=1), (1, n_heads)))


def kernel(x, norm_mix_0, w_in_0, w_proj_a_0, w_proj_b_0, w_out_0, norm_ffn_0, w_gate_0, w_up_0, w_down_0,
           norm_mix_1, w_in_1, w_proj_a_1, w_proj_b_1, w_out_1, norm_ffn_1, w_router_1, w_gate_e_1, w_up_e_1,
           w_down_e_1, norm_final):
    bsz, seq, _ = x.shape
    assert seq % MOBA_BLOCK == 0 and seq % (DIL_STEPS * max(DIL_DILATIONS)) == 0
    cos, sin = _rope_tables(seq)
    row = lambda v: v.reshape(1, D_MODEL).astype(F32)
    xf = x.reshape(bsz * seq, D_MODEL)

    def mixer(xf, nm, w_in, wa, wb, wo):
        proj = _in_proj(xf, row(nm), _prep_w_in(w_in), cos, sin, tm=512)
        oa = _moba(proj, bsz, seq)
        ob = _dilated(proj, bsz, seq)
        return _merge(xf, oa, ob, proj, wa.astype(BF16), wb.astype(BF16), wo.astype(BF16), tm=512)

    xf = mixer(xf, norm_mix_0, w_in_0, w_proj_a_0, w_proj_b_0, w_out_0)
    xf = _swiglu(xf, row(norm_ffn_0), w_gate_0.astype(BF16), w_up_0.astype(BF16), w_down_0.astype(BF16),
                 tm=512, chunk=1024)
    xf = mixer(xf, norm_mix_1, w_in_1, w_proj_a_1, w_proj_b_1, w_out_1)
    wr = jnp.pad(w_router_1.astype(F32), ((0, 0), (0, LANES - N_EXPERTS)))
    xf = _moe(xf, row(norm_ffn_1), wr, w_gate_e_1.astype(BF16), w_up_e_1.astype(BF16),
              w_down_e_1.astype(BF16), row(norm_final))
    return xf.reshape(bsz, seq, D_MODEL)
```

```python
import functools

import jax
import jax.numpy as jnp
import numpy as np
from jax import lax
from jax.experimental import pallas as pl
from jax.experimental.pallas import tpu as pltpu

D_MODEL = 1024
HEAD_DIM = 64
MOBA_HEADS = 8
MOBA_BLOCK = 256
MOBA_TOPK = 3
DIL_DILATIONS = (1, 4, 16)
DIL_STEPS = 128
DIL_HEADS_PER_GROUP = 4
ROPE_THETA = 10000.0
RMS_EPS = 1e-6
N_EXPERTS = 8

WIDTH_A = MOBA_HEADS * HEAD_DIM
WIDTH_BG = DIL_HEADS_PER_GROUP * HEAD_DIM
WIDTH_B = len(DIL_DILATIONS) * WIDTH_BG

LANES = 128
NEG = -1e30
VMEM_LIMIT = 56 * 1024 * 1024

IN_PROJ_TM = 512
MERGE_TM = 1024
SWIGLU_TM = 512
SWIGLU_CHUNK = 1024

COL_GA = 0
COL_GB = COL_GA + D_MODEL
COL_QA = COL_GB + D_MODEL
COL_KA = COL_QA + WIDTH_A
COL_VA = COL_KA + WIDTH_A
COL_QB = COL_VA + WIDTH_A
COL_KB = COL_QB + WIDTH_B
COL_VB = COL_KB + WIDTH_B
PROJ_COLS = COL_VB + WIDTH_B

F32 = jnp.float32
BF16 = jnp.bfloat16


def _params(*sem):
    return pltpu.CompilerParams(dimension_semantics=sem, vmem_limit_bytes=VMEM_LIMIT)


def _dot(a, b):
    return jnp.dot(a, b, preferred_element_type=F32)


def _dot_nt(a, b):
    return lax.dot_general(a, b, (((1,), (1,)), ((), ())), preferred_element_type=F32)


def _rms_norm(x, g):
    return x * lax.rsqrt(jnp.mean(x * x, axis=-1, keepdims=True) + RMS_EPS) * g


def _sigmoid(z):
    return 1.0 / (1.0 + jnp.exp(-z))


def _rope_apply(acc, cos, sin_signed):
    width = acc.shape[-1]
    lane = lax.broadcasted_iota(jnp.int32, acc.shape, 1)
    first_half = (lane % HEAD_DIM) < (HEAD_DIM // 2)
    rot = jnp.where(first_half,
                    pltpu.roll(acc, width - HEAD_DIM // 2, 1),
                    pltpu.roll(acc, HEAD_DIM // 2, 1))
    return acc * cos + rot * sin_signed


def _in_proj_kernel(x_ref, g_ref, w_ref, cos_ref, sin_ref, o_ref):
    h = _rms_norm(x_ref[...], g_ref[...]).astype(BF16)
    plain = ((COL_GA, D_MODEL), (COL_GB, D_MODEL), (COL_VA, WIDTH_A), (COL_VB, WIDTH_B))
    rotary = ((COL_QA, WIDTH_A), (COL_KA, WIDTH_A), (COL_QB, WIDTH_B), (COL_KB, WIDTH_B))
    for c0, w in plain:
        o_ref[:, c0:c0 + w] = _dot(h, w_ref[:, c0:c0 + w]).astype(BF16)
    for c0, w in rotary:
        acc = _dot(h, w_ref[:, c0:c0 + w])
        o_ref[:, c0:c0 + w] = _rope_apply(acc, cos_ref[:, :w], sin_ref[:, :w]).astype(BF16)


def _in_proj(x, g, w, cos, sin, tm):
    n = x.shape[0]
    seq = cos.shape[0]
    n_seq_tiles = seq // tm
    return pl.pallas_call(
        _in_proj_kernel,
        grid=(n // tm,),
        in_specs=[
            pl.BlockSpec((tm, D_MODEL), lambda i: (i, 0)),
            pl.BlockSpec((1, D_MODEL), lambda i: (0, 0)),
            pl.BlockSpec((D_MODEL, PROJ_COLS), lambda i: (0, 0), pipeline_mode=pl.Buffered(1)),
            pl.BlockSpec((tm, WIDTH_B), lambda i: (i % n_seq_tiles, 0)),
            pl.BlockSpec((tm, WIDTH_B), lambda i: (i % n_seq_tiles, 0)),
        ],
        out_specs=pl.BlockSpec((tm, PROJ_COLS), lambda i: (i, 0)),
        out_shape=jax.ShapeDtypeStruct((n, PROJ_COLS), BF16),
        compiler_params=_params("parallel"),
        name="in_proj",
    )(x, g, w, cos, sin)


def _moba_kernel(q_ref, k_ref, v_ref, o_ref, qa_ref, qb_ref, ka_ref, kb_ref, va_ref, vb_ref, *, seq):
    n_blk = seq // MOBA_BLOCK
    q2 = q_ref[...]
    k2 = k_ref[...]
    lane = lax.broadcasted_iota(jnp.int32, (seq, LANES), 1)
    row = lax.broadcasted_iota(jnp.int32, (seq, LANES), 0)
    is_a = lane < HEAD_DIM

    kmean = jnp.mean(k2.astype(F32).reshape(n_blk, MOBA_BLOCK, LANES), axis=1)
    lane8 = lax.broadcasted_iota(jnp.int32, (n_blk, LANES), 1)
    km = jnp.concatenate([jnp.where(lane8 < HEAD_DIM, kmean, 0.0),
                          jnp.where(lane8 >= HEAD_DIM, kmean, 0.0)], axis=0)
    km_hi = km.astype(BF16)
    km_lo = (km - km_hi.astype(F32)).astype(BF16)
    g_all = _dot_nt(jnp.concatenate([km_hi, km_lo], axis=0), q2)
    gates = (g_all[0:n_blk] + g_all[2 * n_blk:3 * n_blk],
             g_all[n_blk:2 * n_blk] + g_all[3 * n_blk:4 * n_blk])

    blk = lax.broadcasted_iota(jnp.int32, (n_blk, seq), 0)
    q_blk = lax.broadcasted_iota(jnp.int32, (n_blk, seq), 1) // MOBA_BLOCK
    past = blk < q_blk
    bias_t = []
    for g in gates:
        g = jnp.where(past, g, -jnp.inf)
        rank = jnp.zeros((n_blk, seq), jnp.int32)
        for jp in range(n_blk):
            gj = g[jp:jp + 1, :]
            beats = (gj > g) | ((gj == g) & (jp < blk))
            rank = rank + beats.astype(jnp.int32)
        keep = ((rank < MOBA_TOPK) & past) | (blk == q_blk)
        bias_t.append(jnp.where(keep, 0.0, NEG))
    pad = jnp.zeros((HEAD_DIM - n_blk, seq), F32)
    bias = jnp.concatenate([bias_t[1], pad, bias_t[0], pad], axis=0).T.astype(BF16)
    key_blk = row // MOBA_BLOCK
    onehot = ((lane == key_blk) | (lane == key_blk + HEAD_DIM)).astype(BF16)

    v2 = v_ref[...]
    ones = jnp.ones_like(v2)
    qa_ref[...] = jnp.where(is_a, q2, bias)
    qb_ref[...] = jnp.where(is_a, bias, q2)
    ka_ref[...] = jnp.where(is_a, k2, onehot)
    kb_ref[...] = jnp.where(is_a, onehot, k2)
    va_ref[...] = jnp.where(is_a, v2, ones)
    vb_ref[...] = jnp.where(is_a, ones, v2)

    tri_r = lax.broadcasted_iota(jnp.int32, (MOBA_BLOCK, MOBA_BLOCK), 0)
    tri_c = lax.broadcasted_iota(jnp.int32, (MOBA_BLOCK, MOBA_BLOCK), 1)
    causal = tri_c <= tri_r
    out_lane = lax.broadcasted_iota(jnp.int32, (MOBA_BLOCK, LANES), 1) < HEAD_DIM

    def rows(j):
        return slice(j * MOBA_BLOCK, (j + 1) * MOBA_BLOCK)

    def half_max(s):
        return jnp.maximum(s[:, :LANES], s[:, LANES:])

    def head(qx_ref, kx_ref, vx_ref, i):
        qx = qx_ref[rows(i), :]
        s_own = jnp.where(causal, _dot_nt(qx, kx_ref[rows(i), :]), NEG)
        m_run = half_max(s_own)
        for j in range(i):
            m_run = jnp.maximum(m_run, half_max(_dot_nt(qx, kx_ref[rows(j), :])))
        m = jnp.max(m_run, axis=1, keepdims=True)
        acc = _dot(jnp.exp(s_own - m).astype(BF16), vx_ref[rows(i), :])
        for j in range(i):
            p = jnp.exp(_dot_nt(qx, kx_ref[rows(j), :]) - m)
            acc = acc + _dot(p.astype(BF16), vx_ref[rows(j), :])
        return acc * (1.0 / pltpu.roll(acc, HEAD_DIM, 1))

    for i in range(n_blk):
        out = jnp.where(out_lane, head(qa_ref, ka_ref, va_ref, i), head(qb_ref, kb_ref, vb_ref, i))
        o_ref[rows(i), :] = out.astype(o_ref.dtype)


def _moba(proj, bsz, seq):
    n = bsz * seq
    qc, kc, vc = COL_QA // LANES, COL_KA // LANES, COL_VA // LANES
    blk = (seq, LANES)
    return pl.pallas_call(
        functools.partial(_moba_kernel, seq=seq),
        grid=(bsz, WIDTH_A // LANES),
        in_specs=[
            pl.BlockSpec(blk, lambda b, h: (b, qc + h)),
            pl.BlockSpec(blk, lambda b, h: (b, kc + h)),
            pl.BlockSpec(blk, lambda b, h: (b, vc + h)),
        ],
        out_specs=pl.BlockSpec(blk, lambda b, h: (b, h)),
        out_shape=jax.ShapeDtypeStruct((n, WIDTH_A), BF16),
        scratch_shapes=[pltpu.VMEM(blk, BF16) for _ in range(6)],
        compiler_params=_params("parallel", "parallel"),
        name="moba",
    )(proj, proj, proj)


def _dilated_kernel(*refs, seq):
    n_groups = len(DIL_DILATIONS)
    in_refs = refs[:3 * n_groups]
    o_ref = refs[3 * n_groups]
    slab_ref, qres_ref, kres_ref, vres_ref, on_ref, ln_ref = refs[3 * n_groups + 1:]
    steps = DIL_STEPS
    n_pairs = WIDTH_BG // LANES
    n_row_blocks = seq // steps

    zeros_pad = jnp.zeros((steps, WIDTH_BG), BF16)
    kres_ref[0:steps, :] = zeros_pad
    vres_ref[0:steps, :] = zeros_pad

    qi = lax.broadcasted_iota(jnp.int32, (steps, 2 * steps), 0)
    kj = lax.broadcasted_iota(jnp.int32, (steps, 2 * steps), 1)
    band = (kj >= qi) & (kj <= qi + steps)
    cur_half = kj >= steps
    is_a = lax.broadcasted_iota(jnp.int32, (steps, LANES), 1) < HEAD_DIM

    for g, d in enumerate(DIL_DILATIONS):
        sub = seq // d
        n_blk = sub // steps
        q_ref, k_ref, v_ref = in_refs[3 * g:3 * g + 3]
        for src, dst, off in ((q_ref, qres_ref, 0), (k_ref, kres_ref, steps), (v_ref, vres_ref, steps)):
            if d == 1:
                dst[off:off + seq, :] = src[...]
                continue
            for hp in range(n_pairs):
                cols = slice(hp * LANES, (hp + 1) * LANES)
                slab_ref[...] = src[:, cols].astype(F32)
                for r in range(d):
                    dst[off + r * sub:off + (r + 1) * sub, cols] = (
                        slab_ref[pl.ds(r, sub, stride=d), :].astype(BF16))

        def row_block(c, _, d=d, n_blk=n_blk, g=g):
            rows_q = pl.ds(pl.multiple_of(c * steps, steps), steps)
            rows_kv = pl.ds(pl.multiple_of(c * steps, steps), 2 * steps)
            prev_valid = (c % n_blk) != 0
            mask = band & (cur_half | prev_valid)
            start = (c % n_blk) * (steps * d) + c // n_blk
            for hp in range(n_pairs):
                cols = slice(hp * LANES, (hp + 1) * LANES)
                q2 = qres_ref[rows_q, cols]
                k2 = kres_ref[rows_kv, cols]
                v2 = vres_ref[rows_kv, cols]
                outs, lses = [], []
                for head_a in (True, False):
                    qm = jnp.where(is_a if head_a else ~is_a, q2, jnp.zeros_like(q2))
                    s = jnp.where(mask, _dot_nt(qm, k2), NEG)
                    m = jnp.max(s, axis=1, keepdims=True)
                    p = jnp.exp(s - m)
                    l = jnp.sum(p, axis=1, keepdims=True)
                    outs.append(_dot(p.astype(BF16), v2) * (1.0 / l))
                    lses.append(jnp.broadcast_to(m + jnp.log(l), (steps, LANES)))
                o2 = jnp.where(is_a, outs[0], outs[1])
                l2 = jnp.where(is_a, lses[0], lses[1])
                slot = g * n_pairs + hp
                if d == 1:
                    on_ref[slot, rows_q, :] = o2
                    ln_ref[slot, rows_q, :] = l2
                else:
                    on_ref[slot, pl.ds(start, steps, stride=d), :] = o2
                    ln_ref[slot, pl.ds(start, steps, stride=d), :] = l2
            return 0

        lax.fori_loop(0, n_row_blocks, row_block, 0, unroll=8)

    for hp in range(n_pairs):
        ls = [ln_ref[g * n_pairs + hp] for g in range(n_groups)]
        mx = functools.reduce(jnp.maximum, ls)
        es = [jnp.exp(l - mx) for l in ls]
        inv = 1.0 / functools.reduce(lambda a, b: a + b, es)
        out = functools.reduce(lambda a, b: a + b,
                               [(es[g] * inv) * on_ref[g * n_pairs + hp] for g in range(n_groups)])
        o_ref[:, hp * LANES:(hp + 1) * LANES] = out.astype(o_ref.dtype)


def _dilated(proj, bsz, seq):
    n = bsz * seq
    n_groups = len(DIL_DILATIONS)
    blk = (seq, WIDTH_BG)
    in_specs = []
    for g in range(n_groups):
        for col in (COL_QB, COL_KB, COL_VB):
            cb = col // WIDTH_BG + g
            in_specs.append(pl.BlockSpec(blk, lambda b, cb=cb: (b, cb)))
    n_slots = n_groups * (WIDTH_BG // LANES)
    return pl.pallas_call(
        functools.partial(_dilated_kernel, seq=seq),
        grid=(bsz,),
        in_specs=in_specs,
        out_specs=pl.BlockSpec(blk, lambda b: (b, 0)),
        out_shape=jax.ShapeDtypeStruct((n, WIDTH_BG), BF16),
        scratch_shapes=[
            pltpu.VMEM((seq, LANES), F32),
            pltpu.VMEM((seq, WIDTH_BG), BF16),
            pltpu.VMEM((seq + DIL_STEPS, WIDTH_BG), BF16),
            pltpu.VMEM((seq + DIL_STEPS, WIDTH_BG), BF16),
            pltpu.VMEM((n_slots, seq, LANES), F32),
            pltpu.VMEM((n_slots, seq, LANES), F32),
        ],
        compiler_params=_params("parallel"),
        name="dilated",
    )(*([proj] * (3 * n_groups)))


def _merge_kernel(x_ref, oa_ref, ob_ref, ga_ref, gb_ref, wa_ref, wb_ref, wo_ref, o_ref):
    ya = _dot(oa_ref[...], wa_ref[...])
    yb = _dot(ob_ref[...], wb_ref[...])
    merged = _sigmoid(ga_ref[...].astype(F32)) * ya + _sigmoid(gb_ref[...].astype(F32)) * yb
    o_ref[...] = x_ref[...] + _dot(merged.astype(BF16), wo_ref[...])


def _merge(x, oa, ob, proj, wa, wb, wo, tm):
    n = x.shape[0]
    const = lambda i: (0, 0)
    return pl.pallas_call(
        _merge_kernel,
        grid=(n // tm,),
        in_specs=[
            pl.BlockSpec((tm, D_MODEL), lambda i: (i, 0)),
            pl.BlockSpec((tm, WIDTH_A), lambda i: (i, 0)),
            pl.BlockSpec((tm, WIDTH_BG), lambda i: (i, 0)),
            pl.BlockSpec((tm, D_MODEL), lambda i: (i, COL_GA // D_MODEL)),
            pl.BlockSpec((tm, D_MODEL), lambda i: (i, COL_GB // D_MODEL)),
            pl.BlockSpec((WIDTH_A, D_MODEL), const),
            pl.BlockSpec((WIDTH_BG, D_MODEL), const),
            pl.BlockSpec((D_MODEL, D_MODEL), const),
        ],
        out_specs=pl.BlockSpec((tm, D_MODEL), lambda i: (i, 0)),
        out_shape=jax.ShapeDtypeStruct((n, D_MODEL), F32),
        compiler_params=_params("parallel"),
        name="merge",
    )(x, oa, ob, proj, proj, wa, wb, wo)


def _swiglu_kernel(x_ref, g_ref, wg_ref, wu_ref, wd_ref, o_ref, *, chunk):
    x = x_ref[...]
    h = _rms_norm(x, g_ref[...]).astype(BF16)
    d_ff = wg_ref.shape[1]
    acc = x
    for c0 in range(0, d_ff, chunk):
        cols = slice(c0, min(c0 + chunk, d_ff))
        a = _dot(h, wg_ref[:, cols])
        b = _dot(h, wu_ref[:, cols])
        t = (a * _sigmoid(a) * b).astype(BF16)
        acc = acc + _dot(t, wd_ref[cols, :])
    o_ref[...] = acc


def _swiglu(x, g, wg, wu, wd, tm, chunk):
    n = x.shape[0]
    d_ff = wg.shape[1]
    const = lambda i: (0, 0)
    return pl.pallas_call(
        functools.partial(_swiglu_kernel, chunk=chunk),
        grid=(n // tm,),
        in_specs=[
            pl.BlockSpec((tm, D_MODEL), lambda i: (i, 0)),
            pl.BlockSpec((1, D_MODEL), const),
            pl.BlockSpec((D_MODEL, d_ff), const, pipeline_mode=pl.Buffered(1)),
            pl.BlockSpec((D_MODEL, d_ff), const, pipeline_mode=pl.Buffered(1)),
            pl.BlockSpec((d_ff, D_MODEL), const, pipeline_mode=pl.Buffered(1)),
        ],
        out_specs=pl.BlockSpec((tm, D_MODEL), lambda i: (i, 0)),
        out_shape=jax.ShapeDtypeStruct((n, D_MODEL), F32),
        compiler_params=_params("parallel"),
        name="swiglu",
    )(x, g, wg, wu, wd)


MOE_TM = 1024
MOE_TG = 256
MOE_TC = 512
MOE_SLOTS = 3
MOE_CSLOTS = 4
MOE_TF = 512
META_E1, META_E2, META_R1, META_R2, META_W1, META_W2 = range(6)


def _route_kernel(x_ref, g_ref, wr_ref, h_ref, meta_ref, metat_ref, cs_ref, tot_ref, carry_ref):
    @pl.when(pl.program_id(0) == 0)
    def _():
        carry_ref[...] = jnp.zeros_like(carry_ref)

    h = _rms_norm(x_ref[...], g_ref[...])
    h_ref[...] = h.astype(BF16)
    logits = jnp.dot(h, wr_ref[...], preferred_element_type=F32, precision=lax.Precision.HIGHEST)
    tc = logits.shape[0]
    lane = lax.broadcasted_iota(jnp.int32, logits.shape, 1)
    logits = jnp.where(lane < N_EXPERTS, logits, -jnp.inf)
    v1 = jnp.max(logits, axis=1, keepdims=True)
    i1 = jnp.min(jnp.where(logits == v1, lane, LANES), axis=1, keepdims=True)
    rest = jnp.where(lane == i1, -jnp.inf, logits)
    v2 = jnp.max(rest, axis=1, keepdims=True)
    i2 = jnp.min(jnp.where(rest == v2, lane, LANES), axis=1, keepdims=True)
    e2 = jnp.exp(v2 - v1)
    w1 = 1.0 / (1.0 + e2)
    w2 = e2 / (1.0 + e2)

    chosen = jnp.where((lane == i1) | (lane == i2), 1.0, 0.0)
    r_i = lax.broadcasted_iota(jnp.int32, (tc, tc), 0)
    c_i = lax.broadcasted_iota(jnp.int32, (tc, tc), 1)
    before = jnp.where(c_i < r_i, 1.0, 0.0).astype(BF16)
    carry = carry_ref[0:1, :]
    rank = _dot(before, chosen.astype(BF16)) + carry
    r1 = jnp.sum(jnp.where(lane == i1, rank, 0.0), axis=1, keepdims=True)
    r2 = jnp.sum(jnp.where(lane == i2, rank, 0.0), axis=1, keepdims=True)
    cols = (i1.astype(F32), i2.astype(F32), r1, r2, w1, w2)
    meta = jnp.zeros(logits.shape, F32)
    for idx, col in enumerate(cols):
        meta = jnp.where(lane == idx, col, meta)
    meta_ref[...] = meta
    metat_ref[...] = meta.T[0:8, :]
    cs_ref[0] = jnp.broadcast_to(carry, (8, LANES))
    total = carry + jnp.sum(chosen, axis=0, keepdims=True)
    carry_ref[...] = jnp.broadcast_to(total, (8, LANES))
    tot_ref[...] = jnp.broadcast_to(total, (8, LANES))


def _route(x, g, wr):
    n = x.shape[0]
    tc = MOE_TC
    n_c = n // tc
    return pl.pallas_call(
        _route_kernel,
        grid=(n_c,),
        in_specs=[
            pl.BlockSpec((tc, D_MODEL), lambda c: (c, 0)),
            pl.BlockSpec((1, D_MODEL), lambda c: (0, 0)),
            pl.BlockSpec((D_MODEL, LANES), lambda c: (0, 0)),
        ],
        out_specs=[
            pl.BlockSpec((tc, D_MODEL), lambda c: (c, 0)),
            pl.BlockSpec((tc, LANES), lambda c: (c, 0)),
            pl.BlockSpec((8, tc), lambda c: (0, c)),
            pl.BlockSpec((1, 8, LANES), lambda c: (c, 0, 0)),
            pl.BlockSpec((8, LANES), lambda c: (0, 0)),
        ],
        out_shape=[
            jax.ShapeDtypeStruct((n, D_MODEL), BF16),
            jax.ShapeDtypeStruct((n, LANES), F32),
            jax.ShapeDtypeStruct((8, n), F32),
            jax.ShapeDtypeStruct((n_c, 8, LANES), F32),
            jax.ShapeDtypeStruct((8, LANES), F32),
        ],
        scratch_shapes=[pltpu.VMEM((8, LANES), F32)],
        compiler_params=_params("arbitrary"),
        name="route",
    )(x, g, wr)


def _moe_plan(cs, tot, n):
    tm, tg, tc = MOE_TM, MOE_TG, MOE_TC
    n_c = n // tc
    n_tiles = 2 * n // tm + N_EXPERTS
    n_g = n_tiles * tm // tg
    k_max = n_g + N_EXPERTS * n_c
    i32 = jnp.int32
    counts = tot[0, :N_EXPERTS].astype(i32)
    cs = cs[:, 0, :N_EXPERTS].astype(i32)
    padded = (counts + tm - 1) // tm * tm
    ends = jnp.cumsum(padded)
    off = ends - padded
    tile_start = jnp.arange(n_tiles, dtype=i32) * tm
    tile_e = jnp.minimum(jnp.sum(tile_start[:, None] >= ends[None, :], axis=1), N_EXPERTS - 1).astype(i32)
    n_valid_tiles = (ends[-1] // tm).astype(i32).reshape(1)

    p0 = jnp.arange(n_g, dtype=i32) * tg
    g_e = tile_e[p0 // tm]
    lo = p0 - off[g_e]
    cs_g = cs[:, g_e].T
    c_lo = jnp.maximum(jnp.sum(cs_g <= lo[:, None], axis=1) - 1, 0)
    c_hi = jnp.maximum(jnp.sum(cs_g < (lo + tg)[:, None], axis=1) - 1, c_lo)
    n_it = c_hi - c_lo + 1
    it_end = jnp.cumsum(n_it)
    it_start = it_end - n_it
    n_items = it_end[-1].astype(i32).reshape(1)
    k = jnp.arange(k_max, dtype=i32)
    valid = k < n_items[0]
    item_g = jnp.minimum(jnp.sum(k[:, None] >= it_end[None, :], axis=1), n_g - 1).astype(i32)
    item_c = jnp.minimum(c_lo[item_g] + k - it_start[item_g], c_hi[item_g]).astype(i32)

    order = jnp.argsort(jnp.where(valid, item_c * n_g + item_g, n_c * n_g + k))
    cc = item_c[order]
    cg = item_g[order]
    cc = jnp.where(valid, cc, n_c).astype(i32)
    cs_n = MOE_CSLOTS
    chunk_ids = jnp.arange(n_c + 1, dtype=i32)
    c_first = jnp.sum(cc[None, :] < chunk_ids[:, None], axis=1).astype(i32)
    m_c = c_first[1:] - c_first[:-1]
    n_ci = (m_c + cs_n - 1) // cs_n
    ci_end = jnp.cumsum(n_ci)
    ci_start = ci_end - n_ci
    n_citems = ci_end[-1].astype(i32).reshape(1)
    kc_max = n_c + (k_max + cs_n - 1) // cs_n
    kc = jnp.arange(kc_max, dtype=i32)
    comb_c = jnp.minimum(jnp.sum(kc[:, None] >= ci_end[None, :], axis=1), n_c - 1).astype(i32)
    run = jnp.minimum(kc - ci_start[comb_c], n_ci[comb_c] - 1)
    comb_first = ((run == 0) & (kc < n_citems[0])).astype(i32)
    comb_last = ((run == n_ci[comb_c] - 1) & (kc < n_citems[0])).astype(i32)
    comb_n = jnp.clip(m_c[comb_c] - run * cs_n, 0, cs_n).astype(i32)
    slot_idx = (c_first[comb_c] + run * cs_n)[None, :] + jnp.arange(cs_n, dtype=i32)[:, None]
    slot_idx = jnp.minimum(slot_idx, (c_first[comb_c] + m_c[comb_c] - 1)[None, :])
    comb_g = cg[slot_idx].astype(i32).reshape(-1)
    slots = MOE_SLOTS
    n_gi = (n_it + slots - 1) // slots
    gi_end = jnp.cumsum(n_gi)
    gi_start = gi_end - n_gi
    n_gitems = gi_end[-1].astype(i32).reshape(1)
    kg_max = n_g + (N_EXPERTS * n_c + slots - 1) // slots
    kg = jnp.arange(kg_max, dtype=i32)
    gat_g = jnp.minimum(jnp.sum(kg[:, None] >= gi_end[None, :], axis=1), n_g - 1).astype(i32)
    gat_hi = c_hi[gat_g].astype(i32)
    gat_c = jnp.minimum(c_lo[gat_g] + (kg - gi_start[gat_g]) * slots, gat_hi).astype(i32)
    gat_first = ((kg == gi_start[gat_g]) & (kg < n_gitems[0])).astype(i32)
    return dict(tile_e=tile_e, n_valid_tiles=n_valid_tiles, g_e=g_e.astype(i32), off=off.astype(i32),
                n_items=n_items, gat_g=gat_g, gat_c=gat_c, gat_hi=gat_hi, gat_first=gat_first,
                n_gitems=n_gitems, kg_max=kg_max,
                comb_c=comb_c, comb_g=comb_g, comb_n=comb_n, comb_first=comb_first, comb_last=comb_last,
                n_citems=n_citems, kc_max=kc_max, n_tiles=n_tiles, n_g=n_g)


def _gather_kernel(ig_ref, ic_ref, ihi_ref, first_ref, n_ref, ge_ref, off_ref, *refs):
    h_refs, mt_refs, o_ref = refs[:MOE_SLOTS], refs[MOE_SLOTS:2 * MOE_SLOTS], refs[2 * MOE_SLOTS]
    k = pl.program_id(0)
    tg, tc = o_ref.shape[0], h_refs[0].shape[0]

    @pl.when(k < n_ref[0])
    def _():
        g = ig_ref[k]
        e = ge_ref[g]
        base = (off_ref[e] - g * tg).astype(F32)
        ef = e.astype(F32)
        row = lax.broadcasted_iota(jnp.int32, (tg, tc), 0).astype(F32)

        @pl.when(first_ref[k] == 1)
        def _():
            o_ref[...] = jnp.zeros_like(o_ref)

        for s in range(MOE_SLOTS):
            @pl.when(ic_ref[k] + s <= ihi_ref[k])
            def _(s=s):
                mt = mt_refs[s][...]
                tgt1 = jnp.where(mt[META_E1:META_E1 + 1] == ef, mt[META_R1:META_R1 + 1] + base, -1.0)
                tgt2 = jnp.where(mt[META_E2:META_E2 + 1] == ef, mt[META_R2:META_R2 + 1] + base, -1.0)
                pick = jnp.where((row == tgt1) | (row == tgt2), 1.0, 0.0).astype(BF16)
                o_ref[...] += _dot(pick, h_refs[s][...]).astype(BF16)


def _gather(plan, h, metat):
    tg, tc = MOE_TG, MOE_TC

    def chunk(k, ic, ihi, s):
        return jnp.minimum(ic[k] + s, ihi[k])

    h_specs = [pl.BlockSpec((tc, D_MODEL), lambda k, ig, ic, ihi, *_, s=s: (chunk(k, ic, ihi, s), 0))
               for s in range(MOE_SLOTS)]
    mt_specs = [pl.BlockSpec((8, tc), lambda k, ig, ic, ihi, *_, s=s: (0, chunk(k, ic, ihi, s)))
                for s in range(MOE_SLOTS)]
    grid_spec = pltpu.PrefetchScalarGridSpec(
        num_scalar_prefetch=7,
        grid=(plan["kg_max"],),
        in_specs=h_specs + mt_specs,
        out_specs=pl.BlockSpec((tg, D_MODEL), lambda k, ig, *_: (ig[k], 0)),
    )
    return pl.pallas_call(
        _gather_kernel,
        grid_spec=grid_spec,
        out_shape=jax.ShapeDtypeStruct((plan["n_g"] * tg, D_MODEL), BF16),
        compiler_params=_params("arbitrary"),
        name="moe_gather",
    )(plan["gat_g"], plan["gat_c"], plan["gat_hi"], plan["gat_first"], plan["n_gitems"], plan["g_e"],
      plan["off"], *([h] * MOE_SLOTS), *([metat] * MOE_SLOTS))


def _ffn_kernel(te_ref, nv_ref, xs_ref, wg_ref, wu_ref, wd_ref, o_ref, acc_ref):
    i = pl.program_id(0)
    f = pl.program_id(1)
    valid = i < nv_ref[0]

    @pl.when(valid & (f == 0))
    def _():
        acc_ref[...] = jnp.zeros_like(acc_ref)

    @pl.when(valid)
    def _():
        xs = xs_ref[...]
        a = _dot(xs, wg_ref[0])
        b = _dot(xs, wu_ref[0])
        t = (a * _sigmoid(a) * b).astype(BF16)
        acc_ref[...] += _dot(t, wd_ref[0])

    last = f == pl.num_programs(1) - 1

    @pl.when(valid & last)
    def _():
        o_ref[...] = acc_ref[...].astype(o_ref.dtype)

    @pl.when(jnp.logical_not(valid) & last)
    def _():
        o_ref[...] = jnp.zeros_like(o_ref)


def _ffn(plan, xs, wg, wu, wd):
    tm, tf = MOE_TM, MOE_TF
    d_ff = wg.shape[2]
    n_f = d_ff // tf

    def f_eff(i, f, nv):
        return jnp.where(i < nv[0], f, n_f - 1)

    grid_spec = pltpu.PrefetchScalarGridSpec(
        num_scalar_prefetch=2,
        grid=(plan["n_tiles"], n_f),
        in_specs=[
            pl.BlockSpec((tm, D_MODEL), lambda i, f, te, nv: (i, 0)),
            pl.BlockSpec((1, D_MODEL, tf), lambda i, f, te, nv: (te[i], 0, f_eff(i, f, nv))),
            pl.BlockSpec((1, D_MODEL, tf), lambda i, f, te, nv: (te[i], 0, f_eff(i, f, nv))),
            pl.BlockSpec((1, tf, D_MODEL), lambda i, f, te, nv: (te[i], f_eff(i, f, nv), 0)),
        ],
        out_specs=pl.BlockSpec((tm, D_MODEL), lambda i, f, te, nv: (i, 0)),
        scratch_shapes=[pltpu.VMEM((tm, D_MODEL), F32)],
    )
    return pl.pallas_call(
        _ffn_kernel,
        grid_spec=grid_spec,
        out_shape=jax.ShapeDtypeStruct(xs.shape, BF16),
        compiler_params=_params("arbitrary", "arbitrary"),
        name="moe_ffn",
    )(plan["tile_e"], plan["n_valid_tiles"], xs, wg, wu, wd)


def _combine_kernel(cc_ref, cg_ref, cn_ref, first_ref, last_ref, n_ref, ge_ref, off_ref,
                    x_ref, meta_ref, *refs):
    y_refs = refs[:MOE_CSLOTS]
    gf_ref, o_ref, acc_ref = refs[MOE_CSLOTS:]
    k = pl.program_id(0)
    kc_max = pl.num_programs(0)
    tg, tc = y_refs[0].shape[0], x_ref.shape[0]

    @pl.when(k < n_ref[0])
    def _():
        meta = meta_ref[...]
        col = lambda c: meta[:, c:c + 1]
        pos = lax.broadcasted_iota(jnp.int32, (tc, tg), 1).astype(F32)
        part = None
        for s in range(MOE_CSLOTS):
            g = cg_ref[s * kc_max + k]
            e = ge_ref[g]
            ef = jnp.where(s < cn_ref[k], e, -1).astype(F32)
            base = (off_ref[e] - g * tg).astype(F32)
            tgt1 = jnp.where(col(META_E1) == ef, col(META_R1) + base, -1.0)
            tgt2 = jnp.where(col(META_E2) == ef, col(META_R2) + base, -1.0)
            spread = (jnp.where(pos == tgt1, col(META_W1), 0.0)
                      + jnp.where(pos == tgt2, col(META_W2), 0.0))
            term = _dot(spread.astype(BF16), y_refs[s][...])
            part = term if part is None else part + term

        @pl.when(first_ref[k] == 1)
        def _():
            acc_ref[...] = part

        @pl.when(first_ref[k] == 0)
        def _():
            acc_ref[...] += part

        @pl.when(last_ref[k] == 1)
        def _():
            o_ref[...] = _rms_norm(x_ref[...] + acc_ref[...], gf_ref[...])


def _combine(plan, x, meta, y, g_final):
    n = x.shape[0]
    tg, tc = MOE_TG, MOE_TC
    kc_max = plan["kc_max"]
    y_specs = [pl.BlockSpec((tg, D_MODEL), lambda k, cc, cg, *_, s=s: (cg[s * kc_max + k], 0))
               for s in range(MOE_CSLOTS)]
    grid_spec = pltpu.PrefetchScalarGridSpec(
        num_scalar_prefetch=8,
        grid=(kc_max,),
        in_specs=[
            pl.BlockSpec((tc, D_MODEL), lambda k, cc, *_: (cc[k], 0)),
            pl.BlockSpec((tc, LANES), lambda k, cc, *_: (cc[k], 0)),
            *y_specs,
            pl.BlockSpec((1, D_MODEL), lambda k, cc, *_: (0, 0)),
        ],
        out_specs=pl.BlockSpec((tc, D_MODEL), lambda k, cc, *_: (cc[k], 0)),
        scratch_shapes=[pltpu.VMEM((tc, D_MODEL), F32)],
    )
    return pl.pallas_call(
        _combine_kernel,
        grid_spec=grid_spec,
        out_shape=jax.ShapeDtypeStruct((n, D_MODEL), F32),
        compiler_params=_params("arbitrary"),
        name="moe_combine",
    )(plan["comb_c"], plan["comb_g"], plan["comb_n"], plan["comb_first"], plan["comb_last"],
      plan["n_citems"], plan["g_e"], plan["off"], x, meta, *([y] * MOE_CSLOTS), g_final)


def _moe(x, g, wr, wg, wu, wd, g_final):
    n = x.shape[0]
    h, meta, metat, cs, tot = _route(x, g, wr)
    plan = _moe_plan(cs, tot, n)
    xs = _gather(plan, h, metat)
    y = _ffn(plan, xs, wg, wu, wd)
    return _combine(plan, x, meta, y, g_final)


def _prep_w_in(w_in):
    sizes = [WIDTH_A] * 3 + [WIDTH_B] * 3 + [D_MODEL] * 2
    qa, ka, va, qb, kb, vb, ga, gb = jnp.split(w_in, np.cumsum(sizes)[:-1].tolist(), axis=1)
    scale = HEAD_DIM ** -0.5
    return jnp.concatenate([ga, gb, qa * scale, ka, va, qb * scale, kb, vb], axis=1).astype(BF16)


def _rope_tables(seq):
    half = HEAD_DIM // 2
    inv = ROPE_THETA ** (-jnp.arange(half, dtype=F32) / half)
    ang = jnp.arange(seq, dtype=F32)[:, None] * inv[None, :]
    cos, sin = jnp.cos(ang), jnp.sin(ang)
    n_heads = WIDTH_B // HEAD_DIM
    return (jnp.tile(jnp.concatenate([cos, cos], axis=1), (1, n_heads)),
            jnp.tile(jnp.concatenate([-sin, sin], axis=1), (1, n_heads)))


def kernel(x, norm_mix_0, w_in_0, w_proj_a_0, w_proj_b_0, w_out_0, norm_ffn_0, w_gate_0, w_up_0, w_down_0,
           norm_mix_1, w_in_1, w_proj_a_1, w_proj_b_1, w_out_1, norm_ffn_1, w_router_1, w_gate_e_1, w_up_e_1,
           w_down_e_1, norm_final):
    bsz, seq, _ = x.shape
    assert seq % MOBA_BLOCK == 0 and seq % (DIL_STEPS * max(DIL_DILATIONS)) == 0
    cos, sin = _rope_tables(seq)
    row = lambda v: v.reshape(1, D_MODEL).astype(F32)
    xf = x.reshape(bsz * seq, D_MODEL)

    def mixer(xf, nm, w_in, wa, wb, wo):
        proj = _in_proj(xf, row(nm), _prep_w_in(w_in), cos, sin, tm=IN_PROJ_TM)
        oa = _moba(proj, bsz, seq)
        ob = _dilated(proj, bsz, seq)
        return _merge(xf, oa, ob, proj, wa.astype(BF16), wb.astype(BF16), wo.astype(BF16), tm=MERGE_TM)

    xf = mixer(xf, norm_mix_0, w_in_0, w_proj_a_0, w_proj_b_0, w_out_0)
    xf = _swiglu(xf, row(norm_ffn_0), w_gate_0.astype(BF16), w_up_0.astype(BF16), w_down_0.astype(BF16),
                 tm=SWIGLU_TM, chunk=SWIGLU_CHUNK)
    xf = mixer(xf, norm_mix_1, w_in_1, w_proj_a_1, w_proj_b_1, w_out_1)
    wr = jnp.pad(w_router_1.astype(F32), ((0, 0), (0, LANES - N_EXPERTS)))
    xf = _moe(xf, row(norm_ffn_1), wr, w_gate_e_1.astype(BF16), w_up_e_1.astype(BF16),
              w_down_e_1.astype(BF16), row(norm_final))
    return xf.reshape(bsz, seq, D_MODEL)
```

```python
import functools

import jax
import jax.numpy as jnp
import numpy as np
from jax import lax
from jax.experimental import pallas as pl
from jax.experimental.pallas import tpu as pltpu

D_MODEL = 1024
HEAD_DIM = 64
MOBA_HEADS = 8
MOBA_BLOCK = 256
MOBA_TOPK = 3
DIL_DILATIONS = (1, 4, 16)
DIL_STEPS = 128
DIL_HEADS_PER_GROUP = 4
ROPE_THETA = 10000.0
RMS_EPS = 1e-6
N_EXPERTS = 8

WIDTH_A = MOBA_HEADS * HEAD_DIM
WIDTH_BG = DIL_HEADS_PER_GROUP * HEAD_DIM
WIDTH_B = len(DIL_DILATIONS) * WIDTH_BG

LANES = 128
NEG = -1e30
VMEM_LIMIT = 56 * 1024 * 1024

IN_PROJ_TM = 512
MERGE_TM = 1024
SWIGLU_TM = 512
SWIGLU_CHUNK = 1024

COL_GA = 0
COL_GB = COL_GA + D_MODEL
COL_QA = COL_GB + D_MODEL
COL_KA = COL_QA + WIDTH_A
COL_VA = COL_KA + WIDTH_A
COL_QB = COL_VA + WIDTH_A
COL_KB = COL_QB + WIDTH_B
COL_VB = COL_KB + WIDTH_B
PROJ_COLS = COL_VB + WIDTH_B

F32 = jnp.float32
BF16 = jnp.bfloat16


def _params(*sem):
    return pltpu.CompilerParams(dimension_semantics=sem, vmem_limit_bytes=VMEM_LIMIT)


def _dot(a, b):
    return jnp.dot(a, b, preferred_element_type=F32)


def _dot_nt(a, b):
    return lax.dot_general(a, b, (((1,), (1,)), ((), ())), preferred_element_type=F32)


def _rms_norm(x, g):
    return x * lax.rsqrt(jnp.mean(x * x, axis=-1, keepdims=True) + RMS_EPS) * g


def _sigmoid(z):
    return 1.0 / (1.0 + jnp.exp(-z))


def _rope_apply(acc, cos, sin_signed):
    width = acc.shape[-1]
    lane = lax.broadcasted_iota(jnp.int32, acc.shape, 1)
    first_half = (lane % HEAD_DIM) < (HEAD_DIM // 2)
    rot = jnp.where(first_half,
                    pltpu.roll(acc, width - HEAD_DIM // 2, 1),
                    pltpu.roll(acc, HEAD_DIM // 2, 1))
    return acc * cos + rot * sin_signed


def _in_proj_kernel(x_ref, g_ref, w_ref, cos_ref, sin_ref, o_ref):
    h = _rms_norm(x_ref[...], g_ref[...]).astype(BF16)
    plain = ((COL_GA, D_MODEL), (COL_GB, D_MODEL), (COL_VA, WIDTH_A), (COL_VB, WIDTH_B))
    rotary = ((COL_QA, WIDTH_A), (COL_KA, WIDTH_A), (COL_QB, WIDTH_B), (COL_KB, WIDTH_B))
    for c0, w in plain:
        o_ref[:, c0:c0 + w] = _dot(h, w_ref[:, c0:c0 + w]).astype(BF16)
    for c0, w in rotary:
        acc = _dot(h, w_ref[:, c0:c0 + w])
        o_ref[:, c0:c0 + w] = _rope_apply(acc, cos_ref[:, :w], sin_ref[:, :w]).astype(BF16)


def _in_proj(x, g, w, cos, sin, tm):
    n = x.shape[0]
    seq = cos.shape[0]
    n_seq_tiles = seq // tm
    return pl.pallas_call(
        _in_proj_kernel,
        grid=(n // tm,),
        in_specs=[
            pl.BlockSpec((tm, D_MODEL), lambda i: (i, 0)),
            pl.BlockSpec((1, D_MODEL), lambda i: (0, 0)),
            pl.BlockSpec((D_MODEL, PROJ_COLS), lambda i: (0, 0), pipeline_mode=pl.Buffered(1)),
            pl.BlockSpec((tm, WIDTH_B), lambda i: (i % n_seq_tiles, 0)),
            pl.BlockSpec((tm, WIDTH_B), lambda i: (i % n_seq_tiles, 0)),
        ],
        out_specs=pl.BlockSpec((tm, PROJ_COLS), lambda i: (i, 0)),
        out_shape=jax.ShapeDtypeStruct((n, PROJ_COLS), BF16),
        compiler_params=_params("parallel"),
        name="in_proj",
    )(x, g, w, cos, sin)


def _moba_kernel(q_ref, k_ref, v_ref, o_ref, qa_ref, qb_ref, ka_ref, kb_ref, va_ref, vb_ref, *, seq):
    n_blk = seq // MOBA_BLOCK
    q2 = q_ref[...]
    k2 = k_ref[...]
    lane = lax.broadcasted_iota(jnp.int32, (seq, LANES), 1)
    row = lax.broadcasted_iota(jnp.int32, (seq, LANES), 0)
    is_a = lane < HEAD_DIM

    kmean = jnp.mean(k2.astype(F32).reshape(n_blk, MOBA_BLOCK, LANES), axis=1)
    lane8 = lax.broadcasted_iota(jnp.int32, (n_blk, LANES), 1)
    km = jnp.concatenate([jnp.where(lane8 < HEAD_DIM, kmean, 0.0),
                          jnp.where(lane8 >= HEAD_DIM, kmean, 0.0)], axis=0)
    km_hi = km.astype(BF16)
    km_lo = (km - km_hi.astype(F32)).astype(BF16)
    g_all = _dot_nt(jnp.concatenate([km_hi, km_lo], axis=0), q2)
    gates = (g_all[0:n_blk] + g_all[2 * n_blk:3 * n_blk],
             g_all[n_blk:2 * n_blk] + g_all[3 * n_blk:4 * n_blk])

    blk = lax.broadcasted_iota(jnp.int32, (n_blk, seq), 0)
    q_blk = lax.broadcasted_iota(jnp.int32, (n_blk, seq), 1) // MOBA_BLOCK
    past = blk < q_blk
    bias_t = []
    for g in gates:
        g = jnp.where(past, g, -jnp.inf)
        rank = jnp.zeros((n_blk, seq), jnp.int32)
        for jp in range(n_blk):
            gj = g[jp:jp + 1, :]
            beats = (gj > g) | ((gj == g) & (jp < blk))
            rank = rank + beats.astype(jnp.int32)
        keep = ((rank < MOBA_TOPK) & past) | (blk == q_blk)
        bias_t.append(jnp.where(keep, 0.0, NEG))
    pad = jnp.zeros((HEAD_DIM - n_blk, seq), F32)
    bias = jnp.concatenate([bias_t[1], pad, bias_t[0], pad], axis=0).T.astype(BF16)
    key_blk = row // MOBA_BLOCK
    onehot = ((lane == key_blk) | (lane == key_blk + HEAD_DIM)).astype(BF16)

    v2 = v_ref[...]
    ones = jnp.ones_like(v2)
    qa_ref[...] = jnp.where(is_a, q2, bias)
    qb_ref[...] = jnp.where(is_a, bias, q2)
    ka_ref[...] = jnp.where(is_a, k2, onehot)
    kb_ref[...] = jnp.where(is_a, onehot, k2)
    va_ref[...] = jnp.where(is_a, v2, ones)
    vb_ref[...] = jnp.where(is_a, ones, v2)

    tri_r = lax.broadcasted_iota(jnp.int32, (MOBA_BLOCK, MOBA_BLOCK), 0)
    tri_c = lax.broadcasted_iota(jnp.int32, (MOBA_BLOCK, MOBA_BLOCK), 1)
    causal = tri_c <= tri_r
    out_lane = lax.broadcasted_iota(jnp.int32, (MOBA_BLOCK, LANES), 1) < HEAD_DIM

    def rows(j):
        return slice(j * MOBA_BLOCK, (j + 1) * MOBA_BLOCK)

    def half_max(s):
        return jnp.maximum(s[:, :LANES], s[:, LANES:])

    def head(qx_ref, kx_ref, vx_ref, i):
        qx = qx_ref[rows(i), :]
        s_own = jnp.where(causal, _dot_nt(qx, kx_ref[rows(i), :]), NEG)
        m_run = half_max(s_own)
        for j in range(i):
            m_run = jnp.maximum(m_run, half_max(_dot_nt(qx, kx_ref[rows(j), :])))
        m = jnp.max(m_run, axis=1, keepdims=True)
        acc = _dot(jnp.exp(s_own - m).astype(BF16), vx_ref[rows(i), :])
        for j in range(i):
            p = jnp.exp(_dot_nt(qx, kx_ref[rows(j), :]) - m)
            acc = acc + _dot(p.astype(BF16), vx_ref[rows(j), :])
        return acc * (1.0 / pltpu.roll(acc, HEAD_DIM, 1))

    for i in range(n_blk):
        out = jnp.where(out_lane, head(qa_ref, ka_ref, va_ref, i), head(qb_ref, kb_ref, vb_ref, i))
        o_ref[rows(i), :] = out.astype(o_ref.dtype)


def _moba(proj, bsz, seq):
    n = bsz * seq
    qc, kc, vc = COL_QA // LANES, COL_KA // LANES, COL_VA // LANES
    blk = (seq, LANES)
    return pl.pallas_call(
        functools.partial(_moba_kernel, seq=seq),
        grid=(bsz, WIDTH_A // LANES),
        in_specs=[
            pl.BlockSpec(blk, lambda b, h: (b, qc + h)),
            pl.BlockSpec(blk, lambda b, h: (b, kc + h)),
            pl.BlockSpec(blk, lambda b, h: (b, vc + h)),
        ],
        out_specs=pl.BlockSpec(blk, lambda b, h: (b, h)),
        out_shape=jax.ShapeDtypeStruct((n, WIDTH_A), BF16),
        scratch_shapes=[pltpu.VMEM(blk, BF16) for _ in range(6)],
        compiler_params=_params("parallel", "parallel"),
        name="moba",
    )(proj, proj, proj)


def _dilated_kernel(*refs, seq):
    n_groups = len(DIL_DILATIONS)
    in_refs = refs[:3 * n_groups]
    o_ref = refs[3 * n_groups]
    slab_ref, qres_ref, kres_ref, vres_ref, on_ref, ln_ref = refs[3 * n_groups + 1:]
    steps = DIL_STEPS
    n_pairs = WIDTH_BG // LANES
    n_row_blocks = seq // steps

    zeros_pad = jnp.zeros((steps, WIDTH_BG), BF16)
    kres_ref[0:steps, :] = zeros_pad
    vres_ref[0:steps, :] = zeros_pad

    qi = lax.broadcasted_iota(jnp.int32, (steps, 2 * steps), 0)
    kj = lax.broadcasted_iota(jnp.int32, (steps, 2 * steps), 1)
    band = (kj >= qi) & (kj <= qi + steps)
    cur_half = kj >= steps
    is_a = lax.broadcasted_iota(jnp.int32, (steps, LANES), 1) < HEAD_DIM

    for g, d in enumerate(DIL_DILATIONS):
        sub = seq // d
        n_blk = sub // steps
        q_ref, k_ref, v_ref = in_refs[3 * g:3 * g + 3]
        for src, dst, off in ((q_ref, qres_ref, 0), (k_ref, kres_ref, steps), (v_ref, vres_ref, steps)):
            if d == 1:
                dst[off:off + seq, :] = src[...]
                continue
            for hp in range(n_pairs):
                cols = slice(hp * LANES, (hp + 1) * LANES)
                slab_ref[...] = src[:, cols].astype(F32)
                for r in range(d):
                    dst[off + r * sub:off + (r + 1) * sub, cols] = (
                        slab_ref[pl.ds(r, sub, stride=d), :].astype(BF16))

        def row_block(c, _, d=d, n_blk=n_blk, g=g):
            rows_q = pl.ds(pl.multiple_of(c * steps, steps), steps)
            rows_kv = pl.ds(pl.multiple_of(c * steps, steps), 2 * steps)
            prev_valid = (c % n_blk) != 0
            mask = band & (cur_half | prev_valid)
            start = (c % n_blk) * (steps * d) + c // n_blk
            for hp in range(n_pairs):
                cols = slice(hp * LANES, (hp + 1) * LANES)
                q2 = qres_ref[rows_q, cols]
                k2 = kres_ref[rows_kv, cols]
                v2 = vres_ref[rows_kv, cols]
                outs, lses = [], []
                for head_a in (True, False):
                    qm = jnp.where(is_a if head_a else ~is_a, q2, jnp.zeros_like(q2))
                    s = jnp.where(mask, _dot_nt(qm, k2), NEG)
                    m = jnp.max(s, axis=1, keepdims=True)
                    p = jnp.exp(s - m)
                    l = jnp.sum(p, axis=1, keepdims=True)
                    outs.append(_dot(p.astype(BF16), v2) * (1.0 / l))
                    lses.append(jnp.broadcast_to(m + jnp.log(l), (steps, LANES)))
                o2 = jnp.where(is_a, outs[0], outs[1])
                l2 = jnp.where(is_a, lses[0], lses[1])
                slot = g * n_pairs + hp
                if d == 1:
                    on_ref[slot, rows_q, :] = o2
                    ln_ref[slot, rows_q, :] = l2
                else:
                    on_ref[slot, pl.ds(start, steps, stride=d), :] = o2
                    ln_ref[slot, pl.ds(start, steps, stride=d), :] = l2
            return 0

        lax.fori_loop(0, n_row_blocks, row_block, 0, unroll=8)

    for hp in range(n_pairs):
        ls = [ln_ref[g * n_pairs + hp] for g in range(n_groups)]
        mx = functools.reduce(jnp.maximum, ls)
        es = [jnp.exp(l - mx) for l in ls]
        inv = 1.0 / functools.reduce(lambda a, b: a + b, es)
        out = functools.reduce(lambda a, b: a + b,
                               [(es[g] * inv) * on_ref[g * n_pairs + hp] for g in range(n_groups)])
        o_ref[:, hp * LANES:(hp + 1) * LANES] = out.astype(o_ref.dtype)


def _dilated(proj, bsz, seq):
    n = bsz * seq
    n_groups = len(DIL_DILATIONS)
    blk = (seq, WIDTH_BG)
    in_specs = []
    for g in range(n_groups):
        for col in (COL_QB, COL_KB, COL_VB):
            cb = col // WIDTH_BG + g
            in_specs.append(pl.BlockSpec(blk, lambda b, cb=cb: (b, cb)))
    n_slots = n_groups * (WIDTH_BG // LANES)
    return pl.pallas_call(
        functools.partial(_dilated_kernel, seq=seq),
        grid=(bsz,),
        in_specs=in_specs,
        out_specs=pl.BlockSpec(blk, lambda b: (b, 0)),
        out_shape=jax.ShapeDtypeStruct((n, WIDTH_BG), BF16),
        scratch_shapes=[
            pltpu.VMEM((seq, LANES), F32),
            pltpu.VMEM((seq, WIDTH_BG), BF16),
            pltpu.VMEM((seq + DIL_STEPS, WIDTH_BG), BF16),
            pltpu.VMEM((seq + DIL_STEPS, WIDTH_BG), BF16),
            pltpu.VMEM((n_slots, seq, LANES), F32),
            pltpu.VMEM((n_slots, seq, LANES), F32),
        ],
        compiler_params=_params("parallel"),
        name="dilated",
    )(*([proj] * (3 * n_groups)))


def _merge_kernel(x_ref, oa_ref, ob_ref, ga_ref, gb_ref, wa_ref, wb_ref, wo_ref, o_ref):
    ya = _dot(oa_ref[...], wa_ref[...])
    yb = _dot(ob_ref[...], wb_ref[...])
    merged = _sigmoid(ga_ref[...].astype(F32)) * ya + _sigmoid(gb_ref[...].astype(F32)) * yb
    o_ref[...] = x_ref[...] + _dot(merged.astype(BF16), wo_ref[...])


def _merge(x, oa, ob, proj, wa, wb, wo, tm):
    n = x.shape[0]
    const = lambda i: (0, 0)
    return pl.pallas_call(
        _merge_kernel,
        grid=(n // tm,),
        in_specs=[
            pl.BlockSpec((tm, D_MODEL), lambda i: (i, 0)),
            pl.BlockSpec((tm, WIDTH_A), lambda i: (i, 0)),
            pl.BlockSpec((tm, WIDTH_BG), lambda i: (i, 0)),
            pl.BlockSpec((tm, D_MODEL), lambda i: (i, COL_GA // D_MODEL)),
            pl.BlockSpec((tm, D_MODEL), lambda i: (i, COL_GB // D_MODEL)),
            pl.BlockSpec((WIDTH_A, D_MODEL), const),
            pl.BlockSpec((WIDTH_BG, D_MODEL), const),
            pl.BlockSpec((D_MODEL, D_MODEL), const),
        ],
        out_specs=pl.BlockSpec((tm, D_MODEL), lambda i: (i, 0)),
        out_shape=jax.ShapeDtypeStruct((n, D_MODEL), F32),
        compiler_params=_params("parallel"),
        name="merge",
    )(x, oa, ob, proj, proj, wa, wb, wo)


def _swiglu_kernel(x_ref, g_ref, wg_ref, wu_ref, wd_ref, o_ref, *, chunk):
    x = x_ref[...]
    h = _rms_norm(x, g_ref[...]).astype(BF16)
    d_ff = wg_ref.shape[1]
    acc = x
    for c0 in range(0, d_ff, chunk):
        cols = slice(c0, min(c0 + chunk, d_ff))
        a = _dot(h, wg_ref[:, cols])
        b = _dot(h, wu_ref[:, cols])
        t = (a * _sigmoid(a) * b).astype(BF16)
        acc = acc + _dot(t, wd_ref[cols, :])
    o_ref[...] = acc


def _swiglu(x, g, wg, wu, wd, tm, chunk):
    n = x.shape[0]
    d_ff = wg.shape[1]
    const = lambda i: (0, 0)
    return pl.pallas_call(
        functools.partial(_swiglu_kernel, chunk=chunk),
        grid=(n // tm,),
        in_specs=[
            pl.BlockSpec((tm, D_MODEL), lambda i: (i, 0)),
            pl.BlockSpec((1, D_MODEL), const),
            pl.BlockSpec((D_MODEL, d_ff), const, pipeline_mode=pl.Buffered(1)),
            pl.BlockSpec((D_MODEL, d_ff), const, pipeline_mode=pl.Buffered(1)),
            pl.BlockSpec((d_ff, D_MODEL), const, pipeline_mode=pl.Buffered(1)),
        ],
        out_specs=pl.BlockSpec((tm, D_MODEL), lambda i: (i, 0)),
        out_shape=jax.ShapeDtypeStruct((n, D_MODEL), F32),
        compiler_params=_params("parallel"),
        name="swiglu",
    )(x, g, wg, wu, wd)


MOE_TM = 1024
MOE_TG = 256
MOE_TC = 512
MOE_SLOTS = 3
MOE_CSLOTS = 4
MOE_TF = 512
META_E1, META_E2, META_R1, META_R2, META_W1, META_W2 = range(6)


def _route_kernel(x_ref, g_ref, wr_ref, h_ref, meta_ref, metat_ref, cs_ref, tot_ref, carry_ref):
    @pl.when(pl.program_id(0) == 0)
    def _():
        carry_ref[...] = jnp.zeros_like(carry_ref)

    h = _rms_norm(x_ref[...], g_ref[...])
    h_hi = h.astype(BF16)
    h_ref[...] = h_hi
    h_lo = (h - h_hi.astype(F32)).astype(BF16)
    wr = wr_ref[...]
    wr_hi = wr.astype(BF16)
    wr_lo = (wr - wr_hi.astype(F32)).astype(BF16)
    logits = _dot(h_hi, wr_hi) + (_dot(h_hi, wr_lo) + _dot(h_lo, wr_hi))
    tc = logits.shape[0]
    lane = lax.broadcasted_iota(jnp.int32, logits.shape, 1)
    logits = jnp.where(lane < N_EXPERTS, logits, -jnp.inf)
    v1 = jnp.max(logits, axis=1, keepdims=True)
    i1 = jnp.min(jnp.where(logits == v1, lane, LANES), axis=1, keepdims=True)
    rest = jnp.where(lane == i1, -jnp.inf, logits)
    v2 = jnp.max(rest, axis=1, keepdims=True)
    i2 = jnp.min(jnp.where(rest == v2, lane, LANES), axis=1, keepdims=True)
    e2 = jnp.exp(v2 - v1)
    w1 = 1.0 / (1.0 + e2)
    w2 = e2 / (1.0 + e2)

    chosen = jnp.where((lane == i1) | (lane == i2), 1.0, 0.0)
    r_i = lax.broadcasted_iota(jnp.int32, (tc, tc), 0)
    c_i = lax.broadcasted_iota(jnp.int32, (tc, tc), 1)
    before = jnp.where(c_i < r_i, 1.0, 0.0).astype(BF16)
    carry = carry_ref[0:1, :]
    rank = _dot(before, chosen.astype(BF16)) + carry
    r1 = jnp.sum(jnp.where(lane == i1, rank, 0.0), axis=1, keepdims=True)
    r2 = jnp.sum(jnp.where(lane == i2, rank, 0.0), axis=1, keepdims=True)
    cols = (i1.astype(F32), i2.astype(F32), r1, r2, w1, w2)
    meta = jnp.zeros(logits.shape, F32)
    for idx, col in enumerate(cols):
        meta = jnp.where(lane == idx, col, meta)
    meta_ref[...] = meta
    metat_ref[...] = meta.T[0:8, :]
    cs_ref[0] = jnp.broadcast_to(carry, (8, LANES))
    total = carry + jnp.sum(chosen, axis=0, keepdims=True)
    carry_ref[...] = jnp.broadcast_to(total, (8, LANES))
    tot_ref[...] = jnp.broadcast_to(total, (8, LANES))


def _route(x, g, wr):
    n = x.shape[0]
    tc = MOE_TC
    n_c = n // tc
    return pl.pallas_call(
        _route_kernel,
        grid=(n_c,),
        in_specs=[
            pl.BlockSpec((tc, D_MODEL), lambda c: (c, 0)),
            pl.BlockSpec((1, D_MODEL), lambda c: (0, 0)),
            pl.BlockSpec((D_MODEL, LANES), lambda c: (0, 0)),
        ],
        out_specs=[
            pl.BlockSpec((tc, D_MODEL), lambda c: (c, 0)),
            pl.BlockSpec((tc, LANES), lambda c: (c, 0)),
            pl.BlockSpec((8, tc), lambda c: (0, c)),
            pl.BlockSpec((1, 8, LANES), lambda c: (c, 0, 0)),
            pl.BlockSpec((8, LANES), lambda c: (0, 0)),
        ],
        out_shape=[
            jax.ShapeDtypeStruct((n, D_MODEL), BF16),
            jax.ShapeDtypeStruct((n, LANES), F32),
            jax.ShapeDtypeStruct((8, n), F32),
            jax.ShapeDtypeStruct((n_c, 8, LANES), F32),
            jax.ShapeDtypeStruct((8, LANES), F32),
        ],
        scratch_shapes=[pltpu.VMEM((8, LANES), F32)],
        compiler_params=_params("arbitrary"),
        name="route",
    )(x, g, wr)


def _moe_plan(cs, tot, n):
    tm, tg, tc = MOE_TM, MOE_TG, MOE_TC
    n_c = n // tc
    n_tiles = 2 * n // tm + N_EXPERTS
    n_g = n_tiles * tm // tg
    k_max = n_g + N_EXPERTS * n_c
    i32 = jnp.int32
    counts = tot[0, :N_EXPERTS].astype(i32)
    cs = cs[:, 0, :N_EXPERTS].astype(i32)
    padded = (counts + tm - 1) // tm * tm
    ends = jnp.cumsum(padded)
    off = ends - padded
    tile_start = jnp.arange(n_tiles, dtype=i32) * tm
    tile_e = jnp.minimum(jnp.sum(tile_start[:, None] >= ends[None, :], axis=1), N_EXPERTS - 1).astype(i32)
    n_valid_tiles = (ends[-1] // tm).astype(i32).reshape(1)

    p0 = jnp.arange(n_g, dtype=i32) * tg
    g_e = tile_e[p0 // tm]
    lo = p0 - off[g_e]
    cs_g = cs[:, g_e].T
    c_lo = jnp.maximum(jnp.sum(cs_g <= lo[:, None], axis=1) - 1, 0)
    c_hi = jnp.maximum(jnp.sum(cs_g < (lo + tg)[:, None], axis=1) - 1, c_lo)
    n_it = c_hi - c_lo + 1
    it_end = jnp.cumsum(n_it)
    it_start = it_end - n_it
    n_items = it_end[-1].astype(i32).reshape(1)
    k = jnp.arange(k_max, dtype=i32)
    valid = k < n_items[0]
    item_g = jnp.minimum(jnp.sum(k[:, None] >= it_end[None, :], axis=1), n_g - 1).astype(i32)
    item_c = jnp.minimum(c_lo[item_g] + k - it_start[item_g], c_hi[item_g]).astype(i32)

    order = jnp.argsort(jnp.where(valid, item_c * n_g + item_g, n_c * n_g + k))
    cc = item_c[order]
    cg = item_g[order]
    cc = jnp.where(valid, cc, n_c).astype(i32)
    cs_n = MOE_CSLOTS
    chunk_ids = jnp.arange(n_c + 1, dtype=i32)
    c_first = jnp.sum(cc[None, :] < chunk_ids[:, None], axis=1).astype(i32)
    m_c = c_first[1:] - c_first[:-1]
    n_ci = (m_c + cs_n - 1) // cs_n
    ci_end = jnp.cumsum(n_ci)
    ci_start = ci_end - n_ci
    n_citems = ci_end[-1].astype(i32).reshape(1)
    kc_max = n_c + (k_max + cs_n - 1) // cs_n
    kc = jnp.arange(kc_max, dtype=i32)
    comb_c = jnp.minimum(jnp.sum(kc[:, None] >= ci_end[None, :], axis=1), n_c - 1).astype(i32)
    run = jnp.minimum(kc - ci_start[comb_c], n_ci[comb_c] - 1)
    comb_first = ((run == 0) & (kc < n_citems[0])).astype(i32)
    comb_last = ((run == n_ci[comb_c] - 1) & (kc < n_citems[0])).astype(i32)
    comb_n = jnp.clip(m_c[comb_c] - run * cs_n, 0, cs_n).astype(i32)
    slot_idx = (c_first[comb_c] + run * cs_n)[None, :] + jnp.arange(cs_n, dtype=i32)[:, None]
    slot_idx = jnp.minimum(slot_idx, (c_first[comb_c] + m_c[comb_c] - 1)[None, :])
    comb_g = cg[slot_idx].astype(i32).reshape(-1)
    slots = MOE_SLOTS
    n_gi = (n_it + slots - 1) // slots
    gi_end = jnp.cumsum(n_gi)
    gi_start = gi_end - n_gi
    n_gitems = gi_end[-1].astype(i32).reshape(1)
    kg_max = n_g + (N_EXPERTS * n_c + slots - 1) // slots
    kg = jnp.arange(kg_max, dtype=i32)
    gat_g = jnp.minimum(jnp.sum(kg[:, None] >= gi_end[None, :], axis=1), n_g - 1).astype(i32)
    gat_hi = c_hi[gat_g].astype(i32)
    gat_c = jnp.minimum(c_lo[gat_g] + (kg - gi_start[gat_g]) * slots, gat_hi).astype(i32)
    gat_first = ((kg == gi_start[gat_g]) & (kg < n_gitems[0])).astype(i32)
    return dict(tile_e=tile_e, n_valid_tiles=n_valid_tiles, g_e=g_e.astype(i32), off=off.astype(i32),
                n_items=n_items, gat_g=gat_g, gat_c=gat_c, gat_hi=gat_hi, gat_first=gat_first,
                n_gitems=n_gitems, kg_max=kg_max,
                comb_c=comb_c, comb_g=comb_g, comb_n=comb_n, comb_first=comb_first, comb_last=comb_last,
                n_citems=n_citems, kc_max=kc_max, n_tiles=n_tiles, n_g=n_g)


def _gather_kernel(ig_ref, ic_ref, ihi_ref, first_ref, n_ref, ge_ref, off_ref, *refs):
    h_refs, mt_refs, o_ref = refs[:MOE_SLOTS], refs[MOE_SLOTS:2 * MOE_SLOTS], refs[2 * MOE_SLOTS]
    k = pl.program_id(0)
    tg, tc = o_ref.shape[0], h_refs[0].shape[0]

    @pl.when(k < n_ref[0])
    def _():
        g = ig_ref[k]
        e = ge_ref[g]
        base = (off_ref[e] - g * tg).astype(F32)
        ef = e.astype(F32)
        row = lax.broadcasted_iota(jnp.int32, (tg, tc), 0).astype(F32)

        @pl.when(first_ref[k] == 1)
        def _():
            o_ref[...] = jnp.zeros_like(o_ref)

        for s in range(MOE_SLOTS):
            @pl.when(ic_ref[k] + s <= ihi_ref[k])
            def _(s=s):
                mt = mt_refs[s][...]
                tgt1 = jnp.where(mt[META_E1:META_E1 + 1] == ef, mt[META_R1:META_R1 + 1] + base, -1.0)
                tgt2 = jnp.where(mt[META_E2:META_E2 + 1] == ef, mt[META_R2:META_R2 + 1] + base, -1.0)
                pick = jnp.where((row == tgt1) | (row == tgt2), 1.0, 0.0).astype(BF16)
                o_ref[...] += _dot(pick, h_refs[s][...]).astype(BF16)


def _gather(plan, h, metat):
    tg, tc = MOE_TG, MOE_TC

    def chunk(k, ic, ihi, s):
        return jnp.minimum(ic[k] + s, ihi[k])

    h_specs = [pl.BlockSpec((tc, D_MODEL), lambda k, ig, ic, ihi, *_, s=s: (chunk(k, ic, ihi, s), 0))
               for s in range(MOE_SLOTS)]
    mt_specs = [pl.BlockSpec((8, tc), lambda k, ig, ic, ihi, *_, s=s: (0, chunk(k, ic, ihi, s)))
                for s in range(MOE_SLOTS)]
    grid_spec = pltpu.PrefetchScalarGridSpec(
        num_scalar_prefetch=7,
        grid=(plan["kg_max"],),
        in_specs=h_specs + mt_specs,
        out_specs=pl.BlockSpec((tg, D_MODEL), lambda k, ig, *_: (ig[k], 0)),
    )
    return pl.pallas_call(
        _gather_kernel,
        grid_spec=grid_spec,
        out_shape=jax.ShapeDtypeStruct((plan["n_g"] * tg, D_MODEL), BF16),
        compiler_params=_params("arbitrary"),
        name="moe_gather",
    )(plan["gat_g"], plan["gat_c"], plan["gat_hi"], plan["gat_first"], plan["n_gitems"], plan["g_e"],
      plan["off"], *([h] * MOE_SLOTS), *([metat] * MOE_SLOTS))


def _ffn_kernel(te_ref, nv_ref, xs_ref, wg_ref, wu_ref, wd_ref, o_ref, acc_ref):
    i = pl.program_id(0)
    f = pl.program_id(1)
    valid = i < nv_ref[0]

    @pl.when(valid & (f == 0))
    def _():
        acc_ref[...] = jnp.zeros_like(acc_ref)

    @pl.when(valid)
    def _():
        xs = xs_ref[...]
        a = _dot(xs, wg_ref[0])
        b = _dot(xs, wu_ref[0])
        t = (a * _sigmoid(a) * b).astype(BF16)
        acc_ref[...] += _dot(t, wd_ref[0])

    last = f == pl.num_programs(1) - 1

    @pl.when(valid & last)
    def _():
        o_ref[...] = acc_ref[...].astype(o_ref.dtype)

    @pl.when(jnp.logical_not(valid) & last)
    def _():
        o_ref[...] = jnp.zeros_like(o_ref)


def _ffn(plan, xs, wg, wu, wd):
    tm, tf = MOE_TM, MOE_TF
    d_ff = wg.shape[2]
    n_f = d_ff // tf

    def f_eff(i, f, nv):
        return jnp.where(i < nv[0], f, n_f - 1)

    grid_spec = pltpu.PrefetchScalarGridSpec(
        num_scalar_prefetch=2,
        grid=(plan["n_tiles"], n_f),
        in_specs=[
            pl.BlockSpec((tm, D_MODEL), lambda i, f, te, nv: (i, 0)),
            pl.BlockSpec((1, D_MODEL, tf), lambda i, f, te, nv: (te[i], 0, f_eff(i, f, nv))),
            pl.BlockSpec((1, D_MODEL, tf), lambda i, f, te, nv: (te[i], 0, f_eff(i, f, nv))),
            pl.BlockSpec((1, tf, D_MODEL), lambda i, f, te, nv: (te[i], f_eff(i, f, nv), 0)),
        ],
        out_specs=pl.BlockSpec((tm, D_MODEL), lambda i, f, te, nv: (i, 0)),
        scratch_shapes=[pltpu.VMEM((tm, D_MODEL), F32)],
    )
    return pl.pallas_call(
        _ffn_kernel,
        grid_spec=grid_spec,
        out_shape=jax.ShapeDtypeStruct(xs.shape, BF16),
        compiler_params=_params("arbitrary", "arbitrary"),
        name="moe_ffn",
    )(plan["tile_e"], plan["n_valid_tiles"], xs, wg, wu, wd)


def _combine_kernel(cc_ref, cg_ref, cn_ref, first_ref, last_ref, n_ref, ge_ref, off_ref,
                    x_ref, meta_ref, *refs):
    y_refs = refs[:MOE_CSLOTS]
    gf_ref, o_ref, acc_ref = refs[MOE_CSLOTS:]
    k = pl.program_id(0)
    kc_max = pl.num_programs(0)
    tg, tc = y_refs[0].shape[0], x_ref.shape[0]

    @pl.when(k < n_ref[0])
    def _():
        meta = meta_ref[...]
        col = lambda c: meta[:, c:c + 1]
        pos = lax.broadcasted_iota(jnp.int32, (tc, tg), 1).astype(F32)
        part = None
        for s in range(MOE_CSLOTS):
            g = cg_ref[s * kc_max + k]
            e = ge_ref[g]
            ef = jnp.where(s < cn_ref[k], e, -1).astype(F32)
            base = (off_ref[e] - g * tg).astype(F32)
            tgt1 = jnp.where(col(META_E1) == ef, col(META_R1) + base, -1.0)
            tgt2 = jnp.where(col(META_E2) == ef, col(META_R2) + base, -1.0)
            spread = (jnp.where(pos == tgt1, col(META_W1), 0.0)
                      + jnp.where(pos == tgt2, col(META_W2), 0.0))
            term = _dot(spread.astype(BF16), y_refs[s][...])
            part = term if part is None else part + term

        @pl.when(first_ref[k] == 1)
        def _():
            acc_ref[...] = part

        @pl.when(first_ref[k] == 0)
        def _():
            acc_ref[...] += part

        @pl.when(last_ref[k] == 1)
        def _():
            o_ref[...] = _rms_norm(x_ref[...] + acc_ref[...], gf_ref[...])


def _combine(plan, x, meta, y, g_final):
    n = x.shape[0]
    tg, tc = MOE_TG, MOE_TC
    kc_max = plan["kc_max"]
    y_specs = [pl.BlockSpec((tg, D_MODEL), lambda k, cc, cg, *_, s=s: (cg[s * kc_max + k], 0))
               for s in range(MOE_CSLOTS)]
    grid_spec = pltpu.PrefetchScalarGridSpec(
        num_scalar_prefetch=8,
        grid=(kc_max,),
        in_specs=[
            pl.BlockSpec((tc, D_MODEL), lambda k, cc, *_: (cc[k], 0)),
            pl.BlockSpec((tc, LANES), lambda k, cc, *_: (cc[k], 0)),
            *y_specs,
            pl.BlockSpec((1, D_MODEL), lambda k, cc, *_: (0, 0)),
        ],
        out_specs=pl.BlockSpec((tc, D_MODEL), lambda k, cc, *_: (cc[k], 0)),
        scratch_shapes=[pltpu.VMEM((tc, D_MODEL), F32)],
    )
    return pl.pallas_call(
        _combine_kernel,
        grid_spec=grid_spec,
        out_shape=jax.ShapeDtypeStruct((n, D_MODEL), F32),
        compiler_params=_params("arbitrary"),
        name="moe_combine",
    )(plan["comb_c"], plan["comb_g"], plan["comb_n"], plan["comb_first"], plan["comb_last"],
      plan["n_citems"], plan["g_e"], plan["off"], x, meta, *([y] * MOE_CSLOTS), g_final)


def _moe(x, g, wr, wg, wu, wd, g_final):
    n = x.shape[0]
    h, meta, metat, cs, tot = _route(x, g, wr)
    plan = _moe_plan(cs, tot, n)
    xs = _gather(plan, h, metat)
    y = _ffn(plan, xs, wg, wu, wd)
    return _combine(plan, x, meta, y, g_final)


def _prep_w_in(w_in):
    sizes = [WIDTH_A] * 3 + [WIDTH_B] * 3 + [D_MODEL] * 2
    qa, ka, va, qb, kb, vb, ga, gb = jnp.split(w_in, np.cumsum(sizes)[:-1].tolist(), axis=1)
    scale = HEAD_DIM ** -0.5
    return jnp.concatenate([ga, gb, qa * scale, ka, va, qb * scale, kb, vb], axis=1).astype(BF16)


def _rope_tables(seq):
    half = HEAD_DIM // 2
    inv = ROPE_THETA ** (-jnp.arange(half, dtype=F32) / half)
    ang = jnp.arange(seq, dtype=F32)[:, None] * inv[None, :]
    cos, sin = jnp.cos(ang), jnp.sin(ang)
    n_heads = WIDTH_B // HEAD_DIM
    return (jnp.tile(jnp.concatenate([cos, cos], axis=1), (1, n_heads)),
            jnp.tile(jnp.concatenate([-sin, sin], axis=1), (1, n_heads)))


def kernel(x, norm_mix_0, w_in_0, w_proj_a_0, w_proj_b_0, w_out_0, norm_ffn_0, w_gate_0, w_up_0, w_down_0,
           norm_mix_1, w_in_1, w_proj_a_1, w_proj_b_1, w_out_1, norm_ffn_1, w_router_1, w_gate_e_1, w_up_e_1,
           w_down_e_1, norm_final):
    bsz, seq, _ = x.shape
    assert seq % MOBA_BLOCK == 0 and seq % (DIL_STEPS * max(DIL_DILATIONS)) == 0
    cos, sin = _rope_tables(seq)
    row = lambda v: v.reshape(1, D_MODEL).astype(F32)
    xf = x.reshape(bsz * seq, D_MODEL)

    def mixer(xf, nm, w_in, wa, wb, wo):
        proj = _in_proj(xf, row(nm), _prep_w_in(w_in), cos, sin, tm=IN_PROJ_TM)
        oa = _moba(proj, bsz, seq)
        ob = _dilated(proj, bsz, seq)
        return _merge(xf, oa, ob, proj, wa.astype(BF16), wb.astype(BF16), wo.astype(BF16), tm=MERGE_TM)

    xf = mixer(xf, norm_mix_0, w_in_0, w_proj_a_0, w_proj_b_0, w_out_0)
    xf = _swiglu(xf, row(norm_ffn_0), w_gate_0.astype(BF16), w_up_0.astype(BF16), w_down_0.astype(BF16),
                 tm=SWIGLU_TM, chunk=SWIGLU_CHUNK)
    xf = mixer(xf, norm_mix_1, w_in_1, w_proj_a_1, w_proj_b_1, w_out_1)
    wr = jnp.pad(w_router_1.astype(F32), ((0, 0), (0, LANES - N_EXPERTS)))
    xf = _moe(xf, row(norm_ffn_1), wr, w_gate_e_1.astype(BF16), w_up_e_1.astype(BF16),
              w_down_e_1.astype(BF16), row(norm_final))
    return xf.reshape(bsz, seq, D_MODEL)
```
